```python
import jax, jax.numpy as jnp
from jax import lax
import numpy as np

D_MODEL = 1024
BATCH = 4
SEQ = 4096
DEPTH = 2
DEC_BATCH = 128
DEC_SEQ = 8
PAST_LEN = 8192
PAGE_SIZE = 128

N_A = DEPTH // 2
N_B = DEPTH - N_A
N_DENSE = (DEPTH + 1) // 2
N_MOE = DEPTH // 2
RET_HEADS = 4
RET_DK = D_MODEL // RET_HEADS
RET_DV = 2 * RET_DK
RET_CHUNK = 128
MLA_HEADS = D_MODEL // 128
MLA_NOPE = 128
MLA_ROPE = 64
MLA_DV = 128
Q_LORA = 3 * D_MODEL // 8
KV_LORA = D_MODEL // 4
Q_BLOCK = 128
D_FF = ((8 * D_MODEL // 3 + 255) // 256) * 256
N_EXPERTS = 8
TOP_K = 2
D_FF_EXPERT = 7 * D_MODEL // 2
ROPE_BASE = 10000.0
EPS = 1e-6

kernel_name = 'yoco_retention_mla_moe_step'

F32 = jnp.float32


def rms_norm(x, g):
    xf = x.astype(F32)
    y = xf * lax.rsqrt(jnp.mean(xf * xf, axis=-1, keepdims=True) + EPS)
    return (y * g.astype(F32)).astype(x.dtype)


def rope_tables(pos, dim):
    inv = ROPE_BASE ** (-jnp.arange(0, dim, 2, dtype=F32) / dim)
    ang = pos.astype(F32)[:, None] * inv[None, :]
    return jnp.cos(ang), jnp.sin(ang)


def apply_rope(x, cos, sin):
    xf = x.astype(F32)
    x1, x2 = jnp.split(xf, 2, axis=-1)
    c = cos[:, None, :]
    s = sin[:, None, :]
    return jnp.concatenate([x1 * c - x2 * s, x1 * s + x2 * c], axis=-1).astype(x.dtype)


def ret_chunk(state, q, k, v, log_gamma):
    C = q.shape[2]
    idx = jnp.arange(C, dtype=F32)
    diff = idx[:, None] - idx[None, :]
    decay = jnp.where(diff >= 0, jnp.exp(log_gamma[:, None, None] * jnp.maximum(diff, 0.0)), 0.0)
    inner = jnp.einsum('bhid,bhjd->bhij', q, k) * decay
    o = jnp.einsum('bhij,bhjv->bhiv', inner, v)
    o = o + jnp.einsum('bhid,bhdv->bhiv', q, state) * jnp.exp(log_gamma[:, None] * (idx + 1.0))[..., None]
    kw = k * jnp.exp(log_gamma[:, None] * (C - 1.0 - idx))[..., None]
    new_state = state * jnp.exp(log_gamma * C)[:, None, None] + jnp.einsum('bhjd,bhjv->bhdv', kw, v)
    return new_state, o


def retention_mixer(xn, state0, pos, w_in, g_out, w_out):
    B, T, _ = xn.shape
    hk, hv = RET_HEADS * RET_DK, RET_HEADS * RET_DV
    proj = xn @ w_in
    q = proj[..., :hk].reshape(B, T, RET_HEADS, RET_DK)
    k = proj[..., hk:2 * hk].reshape(B, T, RET_HEADS, RET_DK)
    v = proj[..., 2 * hk:2 * hk + hv].reshape(B, T, RET_HEADS, RET_DV).astype(F32)
    gate = proj[..., 2 * hk + hv:]
    cos, sin = rope_tables(pos, RET_DK)
    q = apply_rope(q, cos, sin).astype(F32)
    k = apply_rope(k, cos, sin).astype(F32) * (RET_DK ** -0.5)
    c = RET_CHUNK if T % RET_CHUNK == 0 else T
    n = T // c

    def to_chunks(a):
        return a.reshape(B, n, c, RET_HEADS, a.shape[-1]).transpose(1, 0, 3, 2, 4)

    log_gamma = jnp.log1p(-jnp.exp2(-5.0 - jnp.arange(RET_HEADS, dtype=F32)))

    def step(s, blk):
        qc, kc, vc = blk
        return ret_chunk(s, qc, kc, vc, log_gamma)

    s_fin, o = lax.scan(step, state0.astype(F32), (to_chunks(q), to_chunks(k), to_chunks(v)))
    o = o.transpose(1, 0, 3, 2, 4).reshape(B, T, RET_HEADS, RET_DV)
    mu = jnp.mean(o, axis=-1, keepdims=True)
    var = jnp.mean(jnp.square(o - mu), axis=-1, keepdims=True)
    o = ((o - mu) * lax.rsqrt(var + EPS)).reshape(B, T, hv) * g_out.astype(F32)
    y = (jax.nn.silu(gate.astype(F32)) * o).astype(xn.dtype) @ w_out
    return y, s_fin


def mla_kv_latent(h, pos, g_kv_in, w_dkv, g_kv_a, g_kpe):
    xn = rms_norm(h, g_kv_in)
    a = xn @ w_dkv
    ckv = rms_norm(a[..., :KV_LORA], g_kv_a)
    kpe = rms_norm(a[..., KV_LORA:], g_kpe)
    cos, sin = rope_tables(pos, MLA_ROPE)
    kpe = apply_rope(kpe[:, :, None, :], cos, sin)[:, :, 0, :]
    return ckv, kpe


def mla_keys_values(ckv, kpe, w_uk, w_uv, g_kn):
    lead = ckv.shape[:-1]
    k_nope = rms_norm((ckv @ w_uk).reshape(*lead, MLA_HEADS, MLA_NOPE), g_kn)
    k_pe = jnp.broadcast_to(kpe[..., None, :], (*lead, MLA_HEADS, MLA_ROPE)).astype(k_nope.dtype)
    k = jnp.concatenate([k_nope, k_pe], axis=-1)
    v = (ckv @ w_uv).reshape(*lead, MLA_HEADS, MLA_DV)
    return k, v


def mla_queries(xn, pos, w_dq, g_qa, w_uq, g_qn, g_qp):
    B, T, _ = xn.shape
    cq = rms_norm(xn @ w_dq, g_qa)
    q = (cq @ w_uq).reshape(B, T, MLA_HEADS, MLA_NOPE + MLA_ROPE)
    q_nope = rms_norm(q[..., :MLA_NOPE], g_qn)
    cos, sin = rope_tables(pos, MLA_ROPE)
    q_pe = apply_rope(rms_norm(q[..., MLA_NOPE:], g_qp), cos, sin)
    return jnp.concatenate([q_nope, q_pe], axis=-1) * ((MLA_NOPE + MLA_ROPE) ** -0.5)


def mla_prompt_attn(q, k, v):
    B, T, H, dq = q.shape
    nb = T // Q_BLOCK
    qb = q.reshape(B, nb, Q_BLOCK, H, dq).transpose(1, 0, 2, 3, 4)
    kpos = jnp.arange(T)

    def blk(args):
        qi, i = args
        s = jnp.einsum('bqhd,bkhd->bhqk', qi, k).astype(F32)
        qpos = i * Q_BLOCK + jnp.arange(Q_BLOCK)
        s = jnp.where(kpos[None, :] <= qpos[:, None], s, -jnp.inf)
        p = jax.nn.softmax(s, axis=-1)
        return jnp.einsum('bhqk,bkhd->bqhd', p.astype(v.dtype), v)

    o = lax.map(blk, (qb, jnp.arange(nb)))
    return o.transpose(1, 0, 2, 3, 4).reshape(B, T, H, MLA_DV)


def mla_sample_attn(q, ckv_new, kpe_new, page_table, cache_ckv, cache_kpe, w_uk, w_uv, g_kn):
    S = q.shape[1]
    past = page_table.shape[1] * cache_ckv.shape[1]
    mask = jnp.concatenate([jnp.ones((S, past), dtype=bool), jnp.tril(jnp.ones((S, S), dtype=bool))], axis=1)

    def one(args):
        qi, cn, kn, pages = args
        ckv = jnp.concatenate([cache_ckv[pages].reshape(past, KV_LORA), cn.astype(cache_ckv.dtype)], axis=0)
        kpe = jnp.concatenate([cache_kpe[pages].reshape(past, MLA_ROPE), kn.astype(cache_kpe.dtype)], axis=0)
        k, v = mla_keys_values(ckv, kpe, w_uk, w_uv, g_kn)
        s = jnp.einsum('qhd,khd->hqk', qi, k).astype(F32)
        s = jnp.where(mask, s, -jnp.inf)
        p = jax.nn.softmax(s, axis=-1)
        return jnp.einsum('hqk,khd->qhd', p.astype(v.dtype), v)

    return lax.map(one, (q, ckv_new, kpe_new, page_table))


def swiglu(x, w_gu, w_down):
    g, u = jnp.split(x @ w_gu, 2, axis=-1)
    return (jax.nn.silu(g) * u) @ w_down


def moe_ffn(x, w_router, w_gu, w_down):
    shape = x.shape
    xf = x.reshape(-1, shape[-1])
    logits = (xf @ w_router).astype(F32)
    top_v, top_i = lax.top_k(logits, TOP_K)
    gates = jax.nn.softmax(top_v, axis=-1)
    dense_gate = jnp.sum(jax.nn.one_hot(top_i, N_EXPERTS, dtype=F32) * gates[..., None], axis=1)
    y = jnp.zeros(xf.shape, F32)
    for e in range(N_EXPERTS):
        y = y + dense_gate[:, e:e + 1] * swiglu(xf, w_gu[e], w_down[e]).astype(F32)
    return y.astype(x.dtype).reshape(shape)


def trunk(x, pos, ret_init, prepare, attend, g_pre_mix, g_pre_ffn, ret_w_in, ret_g_out, ret_w_out,
          mla_g_kv_in, mla_w_dkv, mla_g_kv_a, mla_g_kpe, mla_w_dq, mla_g_qa, mla_w_uq, mla_g_qn,
          mla_g_qp, mla_w_o, ffn_w_gu, ffn_w_down, moe_w_router, moe_w_gu, moe_w_down):
    B, T, _ = x.shape
    h = x
    ret_final = []
    ckv = kpe = ctx = None
    for layer in range(DEPTH):
        xn = rms_norm(h, g_pre_mix[layer])
        if layer < N_A:
            y, s = retention_mixer(xn, ret_init(layer), pos, ret_w_in[layer], ret_g_out[layer], ret_w_out[layer])
            ret_final.append(s.astype(x.dtype))
            h = h + y
        else:
            if layer == N_A:
                ckv, kpe = mla_kv_latent(xn * 0 + h if False else h, pos, mla_g_kv_in, mla_w_dkv, mla_g_kv_a, mla_g_kpe) if False else mla_kv_latent(h, pos, mla_g_kv_in, mla_w_dkv, mla_g_kv_a, mla_g_kpe)
                ctx = prepare(ckv, kpe)
            j = layer - N_A
            q = mla_queries(xn, pos, mla_w_dq[j], mla_g_qa[j], mla_w_uq[j], mla_g_qn[j], mla_g_qp[j])
            o = attend(q, ctx)
            h = h + o.reshape(B, T, MLA_HEADS * MLA_DV) @ mla_w_o[j]
        xf = rms_norm(h, g_pre_ffn[layer])
        if layer % 2 == 0:
            h = h + swiglu(xf, ffn_w_gu[layer // 2], ffn_w_down[layer // 2])
        else:
            h = h + moe_ffn(xf, moe_w_router[layer // 2], moe_w_gu[layer // 2], moe_w_down[layer // 2])
    return h, jnp.stack(ret_final), ckv, kpe


def setup_inputs(seed: int = 0) -> dict:
    key = jax.random.key(seed)
    ks = iter(jax.random.split(key, 40))

    def nrm(shape, scale):
        return jax.random.normal(next(ks), shape, F32) * scale

    def gain(shape):
        return 1.0 + 0.05 * jax.random.normal(next(ks), shape, F32)

    n_pages = PAST_LEN // PAGE_SIZE
    n_pool = (DEC_BATCH * n_pages * 5 + 3) // 4
    hk, hv = RET_HEADS * RET_DK, RET_HEADS * RET_DV
    d = D_MODEL
    inp = {}
    inp['x_prompt'] = nrm((BATCH, SEQ, d), 1.0)
    inp['x_sample'] = nrm((DEC_BATCH, DEC_SEQ, d), 1.0)
    inp['state_ret'] = nrm((N_A, DEC_BATCH, RET_HEADS, RET_DK, RET_DV), 0.05)
    inp['cache_ckv'] = nrm((n_pool, PAGE_SIZE, KV_LORA), 1.0)
    inp['cache_kpe'] = nrm((n_pool, PAGE_SIZE, MLA_ROPE), 1.0)
    inp['page_table'] = jax.random.permutation(next(ks), n_pool)[:DEC_BATCH * n_pages].reshape(DEC_BATCH, n_pages).astype(jnp.int32)
    inp['g_pre_mix'] = gain((DEPTH, d))
    inp['g_pre_ffn'] = gain((DEPTH, d))
    inp['ret_w_in'] = nrm((N_A, d, 2 * hk + 2 * hv), d ** -0.5)
    inp['ret_g_out'] = gain((N_A, hv))
    inp['ret_w_out'] = nrm((N_A, hv, d), hv ** -0.5)
    inp['mla_g_kv_in'] = gain((d,))
    inp['mla_w_dkv'] = nrm((d, KV_LORA + MLA_ROPE), d ** -0.5)
    inp['mla_g_kv_a'] = gain((KV_LORA,))
    inp['mla_g_kpe'] = gain((MLA_ROPE,))
    inp['mla_w_uk'] = nrm((KV_LORA, MLA_HEADS * MLA_NOPE), KV_LORA ** -0.5)
    inp['mla_w_uv'] = nrm((KV_LORA, MLA_HEADS * MLA_DV), KV_LORA ** -0.5)
    inp['mla_g_kn'] = gain((MLA_NOPE,))
    inp['mla_w_dq'] = nrm((N_B, d, Q_LORA), d ** -0.5)
    inp['mla_g_qa'] = gain((N_B, Q_LORA))
    inp['mla_w_uq'] = nrm((N_B, Q_LORA, MLA_HEADS * (MLA_NOPE + MLA_ROPE)), Q_LORA ** -0.5)
    inp['mla_g_qn'] = gain((N_B, MLA_NOPE))
    inp['mla_g_qp'] = gain((N_B, MLA_ROPE))
    inp['mla_w_o'] = nrm((N_B, MLA_HEADS * MLA_DV, d), (MLA_HEADS * MLA_DV) ** -0.5)
    inp['ffn_w_gu'] = nrm((N_DENSE, d, 2 * D_FF), d ** -0.5)
    inp['ffn_w_down'] = nrm((N_DENSE, D_FF, d), D_FF ** -0.5)
    inp['moe_w_router'] = nrm((N_MOE, d, N_EXPERTS), d ** -0.5)
    inp['moe_w_gu'] = nrm((N_MOE, N_EXPERTS, d, 2 * D_FF_EXPERT), d ** -0.5)
    inp['moe_w_down'] = nrm((N_MOE, N_EXPERTS, D_FF_EXPERT, d), D_FF_EXPERT ** -0.5)
    return inp


def reference(x_prompt, x_sample, state_ret, cache_ckv, cache_kpe, page_table, g_pre_mix, g_pre_ffn,
              ret_w_in, ret_g_out, ret_w_out, mla_g_kv_in, mla_w_dkv, mla_g_kv_a, mla_g_kpe, mla_w_uk,
              mla_w_uv, mla_g_kn, mla_w_dq, mla_g_qa, mla_w_uq, mla_g_qn, mla_g_qp, mla_w_o, ffn_w_gu,
              ffn_w_down, moe_w_router, moe_w_gu, moe_w_down):
    weights = (g_pre_mix, g_pre_ffn, ret_w_in, ret_g_out, ret_w_out, mla_g_kv_in, mla_w_dkv, mla_g_kv_a,
               mla_g_kpe, mla_w_dq, mla_g_qa, mla_w_uq, mla_g_qn, mla_g_qp, mla_w_o, ffn_w_gu, ffn_w_down,
               moe_w_router, moe_w_gu, moe_w_down)
    Bp, Tp, _ = x_prompt.shape
    pos_p = jnp.arange(Tp)
    y_prompt, state_ret_prompt, ckv_prompt, kpe_prompt = trunk(
        x_prompt, pos_p,
        lambda l: jnp.zeros((Bp, RET_HEADS, RET_DK, RET_DV), F32),
        lambda ckv, kpe: mla_keys_values(ckv, kpe, mla_w_uk, mla_w_uv, mla_g_kn),
        lambda q, ctx: mla_prompt_attn(q, ctx[0], ctx[1]),
        *weights)
    past = page_table.shape[1] * cache_ckv.shape[1]
    pos_s = past + jnp.arange(x_sample.shape[1])
    y_sample, state_ret_sample, ckv_sample, kpe_sample = trunk(
        x_sample, pos_s,
        lambda l: state_ret[l],
        lambda ckv, kpe: (ckv, kpe),
        lambda q, ctx: mla_sample_attn(q, ctx[0], ctx[1], page_table, cache_ckv, cache_kpe, mla_w_uk, mla_w_uv, mla_g_kn),
        *weights)
    state_ret_sample = state_ret_sample.astype(state_ret.dtype)
    return (y_prompt, y_sample, state_ret_prompt, state_ret_sample, ckv_prompt, kpe_prompt, ckv_sample, kpe_sample)
```

```python
import functools

import jax
import jax.numpy as jnp
from jax import lax
from jax.experimental import pallas as pl
from jax.experimental.pallas import tpu as pltpu

F32 = jnp.float32
BF16 = jnp.bfloat16
EPS = 1e-6
ROPE_BASE = 10000.0

RET_HEADS = 4
RET_CHUNK = 128
MLA_HEADS = 8
MLA_NOPE = 128
MLA_ROPE = 64
MLA_DV = 128
N_EXPERTS = 8
LANES = 128

TOKEN_TILE = 512
RET_ROWS = 512
ATTN_TILE = 512
EXPERT_TILE = 512
FF_CHUNK = 512
PAGES_PER_CHUNK = 8
MIB = 1024 * 1024


def _params(semantics, vmem_mib):
    return pltpu.CompilerParams(dimension_semantics=semantics, vmem_limit_bytes=vmem_mib * MIB)


def _resident(shape):
    return pl.BlockSpec(shape, lambda *_: (0,) * len(shape), pipeline_mode=pl.Buffered(1))


def _rms(x, g):
    ms = jnp.mean(x * x, axis=-1, keepdims=True)
    return (x * lax.rsqrt(ms + EPS)) * g


def _dot(a, b):
    return jnp.dot(a, b, preferred_element_type=F32)


def _dot_nt(a, b):
    return lax.dot_general(a, b, (((1,), (1,)), ((), ())), preferred_element_type=F32)


def _dot_tn(a, b):
    return lax.dot_general(a, b, (((0,), (0,)), ((), ())), preferred_element_type=F32)


def _silu(x):
    return x * jax.nn.sigmoid(x)


def _ret_in_kernel(x_ref, g_ref, w_ref, cos_ref, sin_ref, q_ref, k_ref, v_ref, gate_ref, *, hk, hv, dk):
    xn = _rms(x_ref[...], g_ref[...]).astype(BF16)
    cos = cos_ref[...]
    sin = sin_ref[...]
    half = dk // 2

    def rope_store(dst, col, acc, scale):
        x1 = acc[:, :half]
        x2 = acc[:, half:]
        dst[:, col:col + half] = ((x1 * cos - x2 * sin) * scale).astype(BF16)
        dst[:, col + half:col + dk] = ((x1 * sin + x2 * cos) * scale).astype(BF16)

    for h in range(hk // dk):
        rope_store(q_ref, h * dk, _dot(xn, w_ref[:, h * dk:(h + 1) * dk]), 1.0)
    for h in range(hk // dk):
        rope_store(k_ref, h * dk, _dot(xn, w_ref[:, hk + h * dk:hk + (h + 1) * dk]), dk ** -0.5)
    step = 512
    for c in range(hv // step):
        v_ref[:, c * step:(c + 1) * step] = _dot(xn, w_ref[:, 2 * hk + c * step:2 * hk + (c + 1) * step]).astype(BF16)
    for c in range(hv // step):
        gate_ref[:, c * step:(c + 1) * step] = _dot(
            xn, w_ref[:, 2 * hk + hv + c * step:2 * hk + hv + (c + 1) * step]).astype(BF16)


def _ret_in(x, g, w, cos_tab, sin_tab, n_prompt_tiles, tiles_per_seq, hk, hv):
    n, d = x.shape
    tm = TOKEN_TILE
    dk = hk // RET_HEADS

    def tab_map(i):
        return (jnp.where(i < n_prompt_tiles, i % tiles_per_seq, tiles_per_seq), 0)

    return pl.pallas_call(
        functools.partial(_ret_in_kernel, hk=hk, hv=hv, dk=dk),
        grid=(n // tm,),
        in_specs=[
            pl.BlockSpec((tm, d), lambda i: (i, 0)),
            _resident((1, d)),
            _resident(w.shape),
            pl.BlockSpec((tm, dk // 2), tab_map),
            pl.BlockSpec((tm, dk // 2), tab_map),
        ],
        out_specs=[
            pl.BlockSpec((tm, hk), lambda i: (i, 0)),
            pl.BlockSpec((tm, hk), lambda i: (i, 0)),
            pl.BlockSpec((tm, hv), lambda i: (i, 0)),
            pl.BlockSpec((tm, hv), lambda i: (i, 0)),
        ],
        out_shape=[
            jax.ShapeDtypeStruct((n, hk), BF16),
            jax.ShapeDtypeStruct((n, hk), BF16),
            jax.ShapeDtypeStruct((n, hv), BF16),
            jax.ShapeDtypeStruct((n, hv), BF16),
        ],
        compiler_params=_params(("parallel",), 48),
        name="ret_in",
    )(x, g, w, cos_tab, sin_tab)


def _group_norm(o):
    mu = jnp.mean(o, axis=-1, keepdims=True)
    var = jnp.mean(jnp.square(o - mu), axis=-1, keepdims=True)
    return (o - mu) * lax.rsqrt(var + EPS)


def _ret_prompt_kernel(q_ref, k_ref, v_ref, dmat_ref, dq_ref, dk_ref, gc_ref, o_ref, sfin_ref, state_ref,
                       *, dk, dv, chunk):
    j = pl.program_id(1)

    @pl.when(j == 0)
    def _():
        state_ref[...] = jnp.zeros_like(state_ref)

    def one_chunk(c, carry):
        r0 = pl.multiple_of(c * chunk, chunk)
        rows = pl.ds(r0, chunk)
        for h in range(RET_HEADS):
            q = q_ref[rows, h * dk:(h + 1) * dk]
            k = k_ref[rows, h * dk:(h + 1) * dk]
            v = v_ref[rows, h * dv:(h + 1) * dv]
            s = state_ref[h]
            inner = _dot_nt(q, k) * dmat_ref[h]
            o = _dot(inner.astype(BF16), v) + _dot(q, s.astype(BF16)) * dq_ref[h]
            kw = (k.astype(F32) * dk_ref[h]).astype(BF16)
            state_ref[h] = s * gc_ref[h] + _dot_tn(kw, v)
            o_ref[rows, h * dv:(h + 1) * dv] = _group_norm(o).astype(BF16)
        return carry

    lax.fori_loop(0, q_ref.shape[0] // chunk, one_chunk, 0)

    @pl.when(j == pl.num_programs(1) - 1)
    def _():
        sfin_ref[0] = state_ref[...]


def _ret_prompt(q, k, v, consts, batch, seq):
    hk = q.shape[1]
    hv = v.shape[1]
    dk, dv = hk // RET_HEADS, hv // RET_HEADS
    rb = RET_ROWS
    per_seq = seq // rb
    dmat, dq, dkk, gc = consts
    row = lambda b, j: (b * per_seq + j, 0)
    return pl.pallas_call(
        functools.partial(_ret_prompt_kernel, dk=dk, dv=dv, chunk=RET_CHUNK),
        grid=(batch, per_seq),
        in_specs=[
            pl.BlockSpec((rb, hk), row),
            pl.BlockSpec((rb, hk), row),
            pl.BlockSpec((rb, hv), row),
            _resident(dmat.shape),
            _resident(dq.shape),
            _resident(dkk.shape),
            _resident(gc.shape),
        ],
        out_specs=[
            pl.BlockSpec((rb, hv), row),
            pl.BlockSpec((1, RET_HEADS, dk, dv), lambda b, j: (b, 0, 0, 0)),
        ],
        out_shape=[
            jax.ShapeDtypeStruct((batch * seq, hv), BF16),
            jax.ShapeDtypeStruct((batch, RET_HEADS, dk, dv), F32),
        ],
        scratch_shapes=[pltpu.VMEM((RET_HEADS, dk, dv), F32)],
        compiler_params=_params(("parallel", "arbitrary"), 40),
        name="ret_prompt",
    )(q, k, v, dmat, dq, dkk, gc)


def _ret_sample_kernel(q_ref, k_ref, v_ref, s_ref, dmat_ref, dq_ref, dk_ref, gc_ref,
                       o_ref, snew_ref, *, dk, dv, dec_seq, seqs):
    rows = seqs * dec_seq
    row_seq = lax.broadcasted_iota(jnp.int32, (rows, 1), 0) // dec_seq
    for h in range(RET_HEADS):
        q = q_ref[:, h * dk:(h + 1) * dk]
        k = k_ref[:, h * dk:(h + 1) * dk]
        v = v_ref[:, h * dv:(h + 1) * dv]
        inner = _dot_nt(q, k) * dmat_ref[h]
        o = _dot(inner.astype(BF16), v)
        kw = k.astype(F32) * dk_ref[h]
        cross = jnp.zeros((rows, dv), F32)
        for s in range(seqs):
            st = s_ref[s, h]
            mine = row_seq == s
            cross = cross + jnp.where(mine, _dot(q, st.astype(BF16)), 0.0)
            kws = jnp.where(mine, kw, 0.0).astype(BF16)
            snew_ref[s, h] = st * gc_ref[h] + _dot_tn(kws, v)
        o = o + cross * dq_ref[h]
        o_ref[:, h * dv:(h + 1) * dv] = _group_norm(o).astype(BF16)


def _ret_sample(q, k, v, state, consts, n_prompt, dec_batch, dec_seq):
    hk = q.shape[1]
    hv = v.shape[1]
    dk, dv = hk // RET_HEADS, hv // RET_HEADS
    seqs = 16 // dec_seq if dec_seq < 16 else 1
    rows = seqs * dec_seq
    base = n_prompt // rows
    dmat, dq, dkk, gc = consts
    row = lambda i: (base + i, 0)
    return pl.pallas_call(
        functools.partial(_ret_sample_kernel, dk=dk, dv=dv, dec_seq=dec_seq, seqs=seqs),
        grid=(dec_batch // seqs,),
        in_specs=[
            pl.BlockSpec((rows, hk), row),
            pl.BlockSpec((rows, hk), row),
            pl.BlockSpec((rows, hv), row),
            pl.BlockSpec((seqs, RET_HEADS, dk, dv), lambda i: (i, 0, 0, 0)),
            _resident(dmat.shape),
            _resident(dq.shape),
            _resident(dkk.shape),
            _resident(gc.shape),
        ],
        out_specs=[
            pl.BlockSpec((rows, hv), lambda i: (i, 0)),
            pl.BlockSpec((seqs, RET_HEADS, dk, dv), lambda i: (i, 0, 0, 0)),
        ],
        out_shape=[
            jax.ShapeDtypeStruct((dec_batch * dec_seq, hv), BF16),
            jax.ShapeDtypeStruct(state.shape, F32),
        ],
        compiler_params=_params(("parallel",), 40),
        name="ret_sample",
    )(q, k, v, state, dmat, dq, dkk, gc)


def _decay_consts(chunk, rows):
    log_gamma = jnp.log1p(-jnp.exp2(-5.0 - jnp.arange(RET_HEADS, dtype=F32)))
    idx = jnp.arange(rows)
    pos = (idx % chunk).astype(F32)
    same = (idx[:, None] // chunk) == (idx[None, :] // chunk)
    diff = pos[:, None] - pos[None, :]
    lg = log_gamma[:, None, None]
    dmat = jnp.where(same & (diff >= 0), jnp.exp(lg * jnp.maximum(diff, 0.0)), 0.0)
    dq = jnp.exp(log_gamma[:, None] * (pos + 1.0))[..., None]
    dk = jnp.exp(log_gamma[:, None] * (chunk - 1.0 - pos))[..., None]
    gc = jnp.exp(log_gamma * chunk)[:, None, None]
    return dmat.astype(F32), dq.astype(F32), dk.astype(F32), gc.astype(F32)


def _mix_ffn_kernel(x_ref, op_ref, os_ref, gate_ref, gout_ref, wout_ref, gffn_ref, wgu_ref, wdown_ref, h_ref,
                    act_ref, *, d_ff, chunk, n_prompt_tiles):
    o = jnp.where(pl.program_id(0) < n_prompt_tiles, op_ref[...], os_ref[...])
    o = o.astype(F32) * gout_ref[...]
    z = (_silu(gate_ref[...].astype(F32)) * o).astype(BF16)
    h1 = x_ref[...] + _dot(z, wout_ref[...])
    xf = _rms(h1, gffn_ref[...]).astype(BF16)
    for c in range(d_ff // chunk):
        g = _dot(xf, wgu_ref[:, c * chunk:(c + 1) * chunk])
        u = _dot(xf, wgu_ref[:, d_ff + c * chunk:d_ff + (c + 1) * chunk])
        act_ref[:, c * chunk:(c + 1) * chunk] = (_silu(g) * u).astype(BF16)
    h_ref[...] = h1 + _dot(act_ref[...], wdown_ref[...])


def _mix_ffn(x, onorm_p, onorm_s, gate, g_out, w_out, g_ffn, w_gu, w_down):
    n, d = x.shape
    hv = onorm_p.shape[1]
    d_ff = w_down.shape[0]
    tm = TOKEN_TILE
    npt = onorm_p.shape[0] // tm
    return pl.pallas_call(
        functools.partial(_mix_ffn_kernel, d_ff=d_ff, chunk=256, n_prompt_tiles=npt),
        grid=(n // tm,),
        in_specs=[
            pl.BlockSpec((tm, d), lambda i: (i, 0)),
            pl.BlockSpec((tm, hv), lambda i: (jnp.minimum(i, npt - 1), 0)),
            pl.BlockSpec((tm, hv), lambda i: (jnp.maximum(i - npt, 0), 0)),
            pl.BlockSpec((tm, hv), lambda i: (i, 0)),
            _resident((1, hv)),
            _resident(w_out.shape),
            _resident((1, d)),
            _resident(w_gu.shape),
            _resident(w_down.shape),
        ],
        out_specs=pl.BlockSpec((tm, d), lambda i: (i, 0)),
        out_shape=jax.ShapeDtypeStruct((n, d), F32),
        scratch_shapes=[pltpu.VMEM((tm, d_ff), BF16)],
        compiler_params=_params(("parallel",), 56),
        name="mix_ffn",
    )(x, onorm_p, onorm_s, gate, g_out, w_out, g_ffn, w_gu, w_down)


def _seg_mean_sq(x, bd_ref, width):
    sq = x * x
    hi = sq.astype(BF16)
    lo = (sq - hi.astype(F32)).astype(BF16)
    return (_dot(hi, bd_ref[...]) + _dot(lo, bd_ref[...])) * (1.0 / width)


def _rope_lanes(x, cos, sin_signed, half):
    width = x.shape[-1]
    lane = lax.broadcasted_iota(jnp.int32, x.shape, 1)
    first = (lane % (2 * half)) < half
    rot = jnp.where(first, pltpu.roll(x, width - half, axis=1), pltpu.roll(x, half, axis=1))
    return x * cos + rot * sin_signed


def _head_rms(x, g, width):
    outs = []
    for h in range(x.shape[-1] // width):
        xs = x[:, h * width:(h + 1) * width]
        ms = jnp.mean(xs * xs, axis=-1, keepdims=True)
        outs.append(xs * lax.rsqrt(ms + EPS))
    return jnp.concatenate(outs, axis=-1) * g


def _mla_proj_kernel(h_ref, gmix_ref, gkv_ref, wdq_ref, gqa_ref, wuqn_ref, wuqp_ref, gqn_ref, gqp_ref,
                     wdkvc_ref, wdkvp_ref, gkva_ref, gkpe_ref, wuk_ref, wuv_ref, gkn_ref, bd_ref,
                     cos_ref, sin_ref,
                     qn_ref, qp_ref, ckv_ref, kpe_ref, kn_ref, vv_ref, *, scale):
    h = h_ref[...]
    r = lax.rsqrt(jnp.mean(h * h, axis=-1, keepdims=True) + EPS)
    xn = ((h * r) * gmix_ref[...]).astype(BF16)
    xk = ((h * r) * gkv_ref[...]).astype(BF16)
    cos = cos_ref[...]
    sin = sin_ref[...]
    half = MLA_ROPE // 2

    cq = _rms(_dot(xn, wdq_ref[...]), gqa_ref[...]).astype(BF16)
    qn = _head_rms(_dot(cq, wuqn_ref[...]), gqn_ref[...], MLA_NOPE)
    qn_ref[...] = (qn * scale).astype(BF16)

    qp = _dot(cq, wuqp_ref[...])
    qp = qp * lax.rsqrt(_seg_mean_sq(qp, bd_ref, MLA_ROPE) + EPS) * gqp_ref[...]
    reps = qp.shape[-1] // LANES
    qp = _rope_lanes(qp, jnp.concatenate([cos] * reps, axis=-1), jnp.concatenate([sin] * reps, axis=-1), half)
    qp_ref[...] = (qp * scale).astype(BF16)

    ckv = _rms(_dot(xk, wdkvc_ref[...]), gkva_ref[...])
    ckv_ref[...] = ckv
    ap = _dot(xk, wdkvp_ref[...])
    ms = jnp.sum(ap * ap, axis=-1, keepdims=True) * (1.0 / MLA_ROPE)
    kpe = _rope_lanes((ap * lax.rsqrt(ms + EPS)) * gkpe_ref[...], cos, sin, half)
    kpe_ref[...] = kpe[:, :MLA_ROPE]

    cb = ckv.astype(BF16)
    kn_ref[...] = _head_rms(_dot(cb, wuk_ref[...]), gkn_ref[...], MLA_NOPE).astype(BF16)
    vv_ref[...] = _dot(cb, wuv_ref[...]).astype(BF16)


def _mla_proj(h, weights, cos_tab, sin_tab, n_prompt_tiles, tiles_per_seq, scale):
    n, d = h.shape
    tm = TOKEN_TILE
    hn = MLA_HEADS * MLA_NOPE
    hp = MLA_HEADS * MLA_ROPE
    kv_lora = weights[8].shape[1]

    def tab_map(i):
        return (jnp.where(i < n_prompt_tiles, i % tiles_per_seq, tiles_per_seq), 0)

    row = lambda i: (i, 0)
    in_specs = [pl.BlockSpec((tm, d), row)] + [_resident(w.shape) for w in weights]
    in_specs += [pl.BlockSpec((tm, LANES), tab_map), pl.BlockSpec((tm, LANES), tab_map)]
    return pl.pallas_call(
        functools.partial(_mla_proj_kernel, scale=scale),
        grid=(n // tm,),
        in_specs=in_specs,
        out_specs=[
            pl.BlockSpec((tm, hn), row),
            pl.BlockSpec((tm, hp), row),
            pl.BlockSpec((tm, kv_lora), row),
            pl.BlockSpec((tm, MLA_ROPE), row),
            pl.BlockSpec((tm, hn), row),
            pl.BlockSpec((tm, hn), row),
        ],
        out_shape=[
            jax.ShapeDtypeStruct((n, hn), BF16),
            jax.ShapeDtypeStruct((n, hp), BF16),
            jax.ShapeDtypeStruct((n, kv_lora), F32),
            jax.ShapeDtypeStruct((n, MLA_ROPE), F32),
            jax.ShapeDtypeStruct((n, hn), BF16),
            jax.ShapeDtypeStruct((n, hn), BF16),
        ],
        compiler_params=_params(("parallel",), 48),
        name="mla_proj",
    )(h, *weights, cos_tab, sin_tab)


def _prompt_attn_kernel(qn_ref, qp_ref, kn_ref, kp_ref, v_ref, o_ref, m_ref, l_ref, acc_ref, *, tile):
    qi = pl.program_id(1)
    ki = pl.program_id(2)

    @pl.when(ki == 0)
    def _():
        m_ref[...] = jnp.full_like(m_ref, -jnp.inf)
        l_ref[...] = jnp.zeros_like(l_ref)
        acc_ref[...] = jnp.zeros_like(acc_ref)

    @pl.when(ki <= qi)
    def _():
        kp = kp_ref[...].astype(BF16)
        row = lax.broadcasted_iota(jnp.int32, (tile, tile), 0)
        col = lax.broadcasted_iota(jnp.int32, (tile, tile), 1)
        visible = (col <= row) | (ki < qi)
        for h in range(MLA_HEADS):
            s = _dot_nt(qn_ref[:, h * MLA_NOPE:(h + 1) * MLA_NOPE], kn_ref[:, h * MLA_NOPE:(h + 1) * MLA_NOPE])
            s = s + _dot_nt(qp_ref[:, h * MLA_ROPE:(h + 1) * MLA_ROPE], kp)
            s = jnp.where(visible, s, -jnp.inf)
            m_old = m_ref[h]
            m_new = jnp.maximum(m_old, jnp.max(s, axis=-1, keepdims=True))
            alpha = jnp.exp(m_old - m_new)
            p = jnp.exp(s - m_new)
            l_ref[h] = alpha * l_ref[h] + jnp.sum(p, axis=-1, keepdims=True)
            m_ref[h] = m_new
            cols = slice(h * MLA_DV, (h + 1) * MLA_DV)
            acc_ref[:, cols] = alpha * acc_ref[:, cols] + _dot(p.astype(BF16), v_ref[:, cols])

    @pl.when(ki == qi)
    def _():
        for h in range(MLA_HEADS):
            cols = slice(h * MLA_DV, (h + 1) * MLA_DV)
            o_ref[:, cols] = acc_ref[:, cols] / l_ref[h]


def _prompt_attn(qn, qp, kn, kpe, vv, batch, seq):
    t = ATTN_TILE
    per_seq = seq // t
    hn = qn.shape[1]
    hp = qp.shape[1]
    qrow = lambda b, qi, ki: (b * per_seq + qi, 0)
    krow = lambda b, qi, ki: (b * per_seq + jnp.minimum(ki, qi), 0)
    return pl.pallas_call(
        functools.partial(_prompt_attn_kernel, tile=t),
        grid=(batch, per_seq, per_seq),
        in_specs=[
            pl.BlockSpec((t, hn), qrow),
            pl.BlockSpec((t, hp), qrow),
            pl.BlockSpec((t, hn), krow),
            pl.BlockSpec((t, MLA_ROPE), krow),
            pl.BlockSpec((t, hn), krow),
        ],
        out_specs=pl.BlockSpec((t, hn), qrow),
        out_shape=jax.ShapeDtypeStruct((batch * seq, hn), F32),
        scratch_shapes=[
            pltpu.VMEM((MLA_HEADS, t, 1), F32),
            pltpu.VMEM((MLA_HEADS, t, 1), F32),
            pltpu.VMEM((t, hn), F32),
        ],
        compiler_params=_params(("parallel", "parallel", "arbitrary"), 40),
        name="prompt_attn",
    )(qn, qp, kn, kpe, vv)


def _absorb_q_kernel(qn_ref, qp_ref, gkn_ref, wuk_ref, o_ref, qps_ref):
    qg = (qn_ref[...].astype(F32) * gkn_ref[...]).astype(BF16)
    lora = wuk_ref.shape[0]
    for h in range(MLA_HEADS):
        cols = slice(h * MLA_NOPE, (h + 1) * MLA_NOPE)
        o_ref[:, h * lora:(h + 1) * lora] = _dot_nt(qg[:, cols], wuk_ref[:, cols])
    qps_ref[...] = qp_ref[...].astype(F32)


def _absorb_q(qn, qp, gkn, wuk, n_prompt, n_sample):
    hn = qn.shape[1]
    hp = qp.shape[1]
    lora = wuk.shape[0]
    tm = min(256, n_sample)
    base = n_prompt // tm
    return pl.pallas_call(
        _absorb_q_kernel,
        grid=(n_sample // tm,),
        in_specs=[
            pl.BlockSpec((tm, hn), lambda i: (base + i, 0)),
            pl.BlockSpec((tm, hp), lambda i: (base + i, 0)),
            _resident(gkn.shape),
            _resident(wuk.shape),
        ],
        out_specs=[
            pl.BlockSpec((tm, MLA_HEADS * lora), lambda i: (i, 0)),
            pl.BlockSpec((tm, hp), lambda i: (i, 0)),
        ],
        out_shape=[
            jax.ShapeDtypeStruct((n_sample, MLA_HEADS * lora), F32),
            jax.ShapeDtypeStruct((n_sample, hp), F32),
        ],
        compiler_params=_params(("parallel",), 32),
        name="absorb_q",
    )(qn, qp, gkn, wuk)


def _sample_attn_kernel(pt_ref, qa_ref, qp_ref, cnew_ref, pnew_ref, wukt_ref, wuv_ref, cache_c_ref, cache_p_ref,
                        o_ref, cbuf, pbuf, sem, m_ref, l_ref, acc_ref, qa_s, qp_s, cnew_s, pnew_s,
                        *, n_pages, ppc, page, dec_seq, lora):
    b = pl.program_id(0)
    nb = pl.num_programs(0)
    n_chunks = n_pages // ppc
    hq = MLA_HEADS * dec_seq

    def copies(seq, chunk, slot):
        out = []
        for p in range(ppc):
            pg = pt_ref[seq, chunk * ppc + p]
            out.append(pltpu.make_async_copy(cache_c_ref.at[pg], cbuf.at[slot, pl.ds(p * page, page)], sem.at[0, slot]))
            out.append(pltpu.make_async_copy(cache_p_ref.at[pg], pbuf.at[slot, pl.ds(p * page, page)], sem.at[1, slot]))
        return out

    @pl.when(b == 0)
    def _():
        for cp in copies(0, 0, 0):
            cp.start()

    for h in range(MLA_HEADS):
        qa_s[h * dec_seq:(h + 1) * dec_seq, :] = qa_ref[:, h * lora:(h + 1) * lora].astype(F32)
        qp_s[h * dec_seq:(h + 1) * dec_seq, :] = qp_ref[:, h * MLA_ROPE:(h + 1) * MLA_ROPE].astype(F32)
    qa = qa_s[...].astype(BF16)
    qp = qp_s[...].astype(BF16)
    m_ref[...] = jnp.full_like(m_ref, -jnp.inf)
    l_ref[...] = jnp.zeros_like(l_ref)
    acc_ref[...] = jnp.zeros_like(acc_ref)

    def attend(cb, pb, mask):
        kt = _dot_nt(wukt_ref[...], cb)
        raw = _dot_nt(qa, cb)
        parts = []
        for h in range(MLA_HEADS):
            kh = kt[h * MLA_NOPE:(h + 1) * MLA_NOPE, :]
            ms = jnp.sum(kh * kh, axis=0, keepdims=True) * (1.0 / MLA_NOPE)
            parts.append(raw[h * dec_seq:(h + 1) * dec_seq, :] * lax.rsqrt(ms + EPS))
        s = jnp.concatenate(parts, axis=0) + _dot_nt(qp, pb)
        if mask is not None:
            s = jnp.where(mask, s, -jnp.inf)
        m_old = m_ref[...]
        m_new = jnp.maximum(m_old, jnp.max(s, axis=-1, keepdims=True))
        alpha = jnp.exp(m_old - m_new)
        p = jnp.exp(s - m_new)
        l_ref[...] = alpha * l_ref[...] + jnp.sum(p, axis=-1, keepdims=True)
        m_ref[...] = m_new
        acc_ref[...] = alpha * acc_ref[...] + _dot(p.astype(BF16), cb)

    def chunk_body(c, carry):
        slot = (b * n_chunks + c) % 2
        nxt = 1 - slot

        @pl.when(c + 1 < n_chunks)
        def _():
            for cp in copies(b, c + 1, nxt):
                cp.start()

        @pl.when((c + 1 == n_chunks) & (b + 1 < nb))
        def _():
            for cp in copies(b + 1, 0, nxt):
                cp.start()

        for cp in copies(b, c, slot):
            cp.wait()
        attend(cbuf[slot].astype(BF16), pbuf[slot].astype(BF16), None)
        return carry

    lax.fori_loop(0, n_chunks, chunk_body, 0)

    cnew_s[...] = jnp.zeros_like(cnew_s)
    pnew_s[...] = jnp.zeros_like(pnew_s)
    cnew_s[0:dec_seq, :] = cnew_ref[...]
    pnew_s[0:dec_seq, :] = pnew_ref[...]
    row_tok = lax.broadcasted_iota(jnp.int32, (hq, LANES), 0) % dec_seq
    col_tok = lax.broadcasted_iota(jnp.int32, (hq, LANES), 1)
    attend(cnew_s[...].astype(BF16), pnew_s[...].astype(BF16), col_tok <= row_tok)

    lat = (acc_ref[...] / l_ref[...]).astype(BF16)
    full = _dot(lat, wuv_ref[...])
    for h in range(MLA_HEADS):
        cols = slice(h * MLA_DV, (h + 1) * MLA_DV)
        o_ref[:, cols] = full[h * dec_seq:(h + 1) * dec_seq, cols]


def _sample_attn(page_table, qa, qp, ckv, kpe, wukt, wuv, cache_c, cache_p, n_prompt, dec_batch, dec_seq):
    n_pages = page_table.shape[1]
    page = cache_c.shape[1]
    lora = cache_c.shape[2]
    ppc = min(PAGES_PER_CHUNK, n_pages)
    hn = MLA_HEADS * MLA_DV
    hq = MLA_HEADS * dec_seq
    base = n_prompt // dec_seq
    srow = lambda b, pt: (b, 0)
    nrow = lambda b, pt: (base + b, 0)
    grid_spec = pltpu.PrefetchScalarGridSpec(
        num_scalar_prefetch=1,
        grid=(dec_batch,),
        in_specs=[
            pl.BlockSpec((dec_seq, MLA_HEADS * lora), srow),
            pl.BlockSpec((dec_seq, MLA_HEADS * MLA_ROPE), srow),
            pl.BlockSpec((dec_seq, lora), nrow),
            pl.BlockSpec((dec_seq, MLA_ROPE), nrow),
            pl.BlockSpec(wukt.shape, lambda b, pt: (0, 0), pipeline_mode=pl.Buffered(1)),
            pl.BlockSpec(wuv.shape, lambda b, pt: (0, 0), pipeline_mode=pl.Buffered(1)),
            pl.BlockSpec(memory_space=pl.ANY),
            pl.BlockSpec(memory_space=pl.ANY),
        ],
        out_specs=pl.BlockSpec((dec_seq, hn), srow),
        scratch_shapes=[
            pltpu.VMEM((2, ppc * page, lora), F32),
            pltpu.VMEM((2, ppc * page, MLA_ROPE), F32),
            pltpu.SemaphoreType.DMA((2, 2)),
            pltpu.VMEM((hq, 1), F32),
            pltpu.VMEM((hq, 1), F32),
            pltpu.VMEM((hq, lora), F32),
            pltpu.VMEM((hq, lora), F32),
            pltpu.VMEM((hq, MLA_ROPE), F32),
            pltpu.VMEM((LANES, lora), F32),
            pltpu.VMEM((LANES, MLA_ROPE), F32),
        ],
    )
    return pl.pallas_call(
        functools.partial(_sample_attn_kernel, n_pages=n_pages, ppc=ppc, page=page, dec_seq=dec_seq, lora=lora),
        grid_spec=grid_spec,
        out_shape=jax.ShapeDtypeStruct((dec_batch * dec_seq, hn), F32),
        compiler_params=_params(("arbitrary",), 40),
        name="sample_attn",
    )(page_table, qa, qp, ckv, kpe, wukt, wuv, cache_c, cache_p)


def _route_kernel(h_ref, op_ref, os_ref, wo_ref, gffn_ref, wr_ref, tri_ref, h3_ref, xf_ref, info_ref, infot_ref,
                  cnt_ref, carry_ref, *, n_prompt_tiles):
    i = pl.program_id(0)

    @pl.when(i == 0)
    def _():
        carry_ref[...] = jnp.zeros_like(carry_ref)

    o = jnp.where(i < n_prompt_tiles, op_ref[...], os_ref[...])
    h3 = h_ref[...] + _dot(o.astype(BF16), wo_ref[...])
    h3_ref[...] = h3
    xf = _rms(h3, gffn_ref[...])
    xf_ref[...] = xf
    logits = _dot(xf.astype(BF16), wr_ref[...])
    lane = lax.broadcasted_iota(jnp.int32, logits.shape, 1).astype(F32)
    lg = jnp.where(lane < N_EXPERTS, logits, -jnp.inf)
    v1 = jnp.max(lg, axis=-1, keepdims=True)
    i1 = jnp.min(jnp.where(lg == v1, lane, float(LANES)), axis=-1, keepdims=True)
    lg2 = jnp.where(lane == i1, -jnp.inf, lg)
    v2 = jnp.max(lg2, axis=-1, keepdims=True)
    i2 = jnp.min(jnp.where(lg2 == v2, lane, float(LANES)), axis=-1, keepdims=True)
    e = jnp.exp(v2 - v1)
    g1 = 1.0 / (1.0 + e)
    g2 = e / (1.0 + e)
    oh1 = lane == i1
    oh2 = lane == i2
    chosen = jnp.where(oh1 | oh2, 1.0, 0.0)
    cum = _dot(tri_ref[...], chosen.astype(BF16))
    before = cum - chosen + carry_ref[...]
    pos1 = jnp.sum(jnp.where(oh1, before, 0.0), axis=-1, keepdims=True)
    pos2 = jnp.sum(jnp.where(oh2, before, 0.0), axis=-1, keepdims=True)
    carry_ref[...] = carry_ref[...] + cum[cum.shape[0] - 1:, :]
    cnt_ref[...] = carry_ref[...]
    info = jnp.where(lane == 0, i1, 0.0)
    info = jnp.where(lane == 1, i2, info)
    info = jnp.where(lane == 2, pos1, info)
    info = jnp.where(lane == 3, pos2, info)
    info = jnp.where(lane == 4, g1, info)
    info = jnp.where(lane == 5, g2, info)
    info_ref[...] = info
    infot_ref[...] = info.T[:8, :]


def _route(h2, o_attn_p, o_attn_s, w_o, g_ffn, w_r, tri):
    n, d = h2.shape
    tm = TOKEN_TILE
    npt = o_attn_p.shape[0] // tm
    hn = o_attn_p.shape[1]
    row = lambda i: (i, 0)
    return pl.pallas_call(
        functools.partial(_route_kernel, n_prompt_tiles=npt),
        grid=(n // tm,),
        in_specs=[
            pl.BlockSpec((tm, d), row),
            pl.BlockSpec((tm, hn), lambda i: (jnp.minimum(i, npt - 1), 0)),
            pl.BlockSpec((tm, hn), lambda i: (jnp.maximum(i - npt, 0), 0)),
            _resident(w_o.shape),
            _resident((1, d)),
            _resident(w_r.shape),
            _resident(tri.shape),
        ],
        out_specs=[
            pl.BlockSpec((tm, d), row),
            pl.BlockSpec((tm, d), row),
            pl.BlockSpec((tm, LANES), row),
            pl.BlockSpec((8, tm), lambda i: (0, i)),
            pl.BlockSpec((1, LANES), lambda i: (0, 0)),
        ],
        out_shape=[
            jax.ShapeDtypeStruct((n, d), F32),
            jax.ShapeDtypeStruct((n, d), F32),
            jax.ShapeDtypeStruct((n, LANES), F32),
            jax.ShapeDtypeStruct((8, n), F32),
            jax.ShapeDtypeStruct((1, LANES), F32),
        ],
        scratch_shapes=[pltpu.VMEM((1, LANES), F32)],
        compiler_params=_params(("arbitrary",), 40),
        name="route",
    )(h2, o_attn_p, o_attn_s, w_o, g_ffn, w_r, tri)


def _scatter_kernel(dest_ref, x_ref, xs_in_ref, xs_ref, sem, *, tm):
    del xs_in_ref

    def issue(r, carry):
        for k in range(2):
            d = dest_ref[0, 0, k * tm + r]
            pltpu.make_async_copy(x_ref.at[pl.ds(r, 1)], xs_ref.at[pl.ds(d, 1)], sem).start()
        return carry

    lax.fori_loop(0, tm, issue, 0)

    def drain(r, carry):
        for k in range(2):
            pltpu.make_async_copy(x_ref.at[pl.ds(r, 1)], xs_ref.at[pl.ds(0, 1)], sem).wait()
        return carry

    lax.fori_loop(0, tm, drain, 0)


def _scatter_rows(dest, xf, xs0):
    n, d = xf.shape
    tm = TOKEN_TILE
    return pl.pallas_call(
        functools.partial(_scatter_kernel, tm=tm),
        grid=(n // tm,),
        in_specs=[
            pl.BlockSpec((1, 1, 2 * tm), lambda i: (i, 0, 0), memory_space=pltpu.SMEM),
            pl.BlockSpec((tm, d), lambda i: (i, 0)),
            pl.BlockSpec(memory_space=pl.ANY),
        ],
        out_specs=pl.BlockSpec(memory_space=pl.ANY),
        out_shape=jax.ShapeDtypeStruct(xs0.shape, xs0.dtype),
        scratch_shapes=[pltpu.SemaphoreType.DMA(())],
        input_output_aliases={2: 0},
        compiler_params=_params(("arbitrary",), 32),
        name="scatter_rows",
    )(dest, xf, xs0)


def _experts_kernel(te_ref, tv_ref, x_ref, wg_ref, wu_ref, wd_ref, o_ref, acc_ref, xb_ref):
    t = pl.program_id(0)
    c = pl.program_id(1)

    @pl.when(tv_ref[t] == 1)
    def _():
        @pl.when(c == 0)
        def _():
            xb_ref[...] = x_ref[...].astype(BF16)
            acc_ref[...] = jnp.zeros_like(acc_ref)

        xb = xb_ref[...]
        act = (_silu(_dot(xb, wg_ref[0])) * _dot(xb, wu_ref[0])).astype(BF16)
        acc_ref[...] += _dot(act, wd_ref[0])

        @pl.when(c == pl.num_programs(1) - 1)
        def _():
            o_ref[...] = acc_ref[...]

    @pl.when((tv_ref[t] == 0) & (c == 0))
    def _():
        o_ref[...] = jnp.zeros_like(o_ref)


def _experts(tile_expert, tile_valid, xs, w_gu, w_down):
    rows, d = xs.shape
    tm = EXPERT_TILE
    fc = FF_CHUNK
    d_ff = w_down.shape[1]
    n_c = d_ff // fc
    n_tiles = rows // tm

    def ceff(t, c, tv):
        return jnp.where(tv[t] == 1, c, n_c - 1)

    grid_spec = pltpu.PrefetchScalarGridSpec(
        num_scalar_prefetch=2,
        grid=(n_tiles, n_c),
        in_specs=[
            pl.BlockSpec((tm, d), lambda t, c, te, tv: (t, 0)),
            pl.BlockSpec((1, d, fc), lambda t, c, te, tv: (te[t], 0, ceff(t, c, tv))),
            pl.BlockSpec((1, d, fc), lambda t, c, te, tv: (te[t], 0, n_c + ceff(t, c, tv))),
            pl.BlockSpec((1, fc, d), lambda t, c, te, tv: (te[t], ceff(t, c, tv), 0)),
        ],
        out_specs=pl.BlockSpec((tm, d), lambda t, c, te, tv: (t, 0)),
        scratch_shapes=[pltpu.VMEM((tm, d), F32), pltpu.VMEM((tm, d), BF16)],
    )
    return pl.pallas_call(
        _experts_kernel,
        grid_spec=grid_spec,
        out_shape=jax.ShapeDtypeStruct((rows, d), F32),
        compiler_params=_params(("arbitrary", "arbitrary"), 40),
        name="experts",
    )(tile_expert, tile_valid, xs, w_gu, w_gu, w_down)


def _combine_kernel(dest_ref, h_ref, info_ref, ys_ref, y_ref, buf, sem, *, tm):
    def issue(r, carry):
        for k in range(2):
            d = dest_ref[0, 0, k * tm + r]
            pltpu.make_async_copy(ys_ref.at[pl.ds(d, 1)], buf.at[k, pl.ds(r, 1)], sem).start()
        return carry

    lax.fori_loop(0, tm, issue, 0)

    def drain(r, carry):
        for k in range(2):
            pltpu.make_async_copy(ys_ref.at[pl.ds(0, 1)], buf.at[k, pl.ds(r, 1)], sem).wait()
        return carry

    lax.fori_loop(0, tm, drain, 0)
    info = info_ref[...]
    y_ref[...] = h_ref[...] + info[:, 4:5] * buf[0] + info[:, 5:6] * buf[1]


def _combine(dest, h3, info, ys):
    n, d = h3.shape
    tm = TOKEN_TILE
    return pl.pallas_call(
        functools.partial(_combine_kernel, tm=tm),
        grid=(n // tm,),
        in_specs=[
            pl.BlockSpec((1, 1, 2 * tm), lambda i: (i, 0, 0), memory_space=pltpu.SMEM),
            pl.BlockSpec((tm, d), lambda i: (i, 0)),
            pl.BlockSpec((tm, LANES), lambda i: (i, 0)),
            pl.BlockSpec(memory_space=pl.ANY),
        ],
        out_specs=pl.BlockSpec((tm, d), lambda i: (i, 0)),
        out_shape=jax.ShapeDtypeStruct((n, d), F32),
        scratch_shapes=[pltpu.VMEM((2, tm, d), F32), pltpu.SemaphoreType.DMA(())],
        compiler_params=_params(("arbitrary",), 32),
        name="combine",
    )(dest, h3, info, ys)


def _rope_tables(pos, dim):
    inv = ROPE_BASE ** (-jnp.arange(0, dim, 2, dtype=F32) / dim)
    ang = pos.astype(F32)[:, None] * inv[None, :]
    return jnp.cos(ang), jnp.sin(ang)


def kernel(x_prompt, x_sample, state_ret, cache_ckv, cache_kpe, page_table, g_pre_mix, g_pre_ffn, ret_w_in, ret_g_out, ret_w_out, mla_g_kv_in, mla_w_dkv, mla_g_kv_a, mla_g_kpe, mla_w_uk, mla_w_uv, mla_g_kn, mla_w_dq, mla_g_qa, mla_w_uq, mla_g_qn, mla_g_qp, mla_w_o, ffn_w_gu, ffn_w_down, moe_w_router, moe_w_gu, moe_w_down):
    batch, seq, d = x_prompt.shape
    dec_batch, dec_seq, _ = x_sample.shape
    n_prompt = batch * seq
    n_sample = dec_batch * dec_seq
    n = n_prompt + n_sample
    tm = TOKEN_TILE
    past = page_table.shape[1] * cache_ckv.shape[1]
    hv = ret_g_out.shape[1]
    hk = (ret_w_in.shape[2] - 2 * hv) // 2
    kv_lora = mla_g_kv_a.shape[0]
    assert seq % tm == 0 and n_sample % tm == 0 and seq % ATTN_TILE == 0 and seq % RET_ROWS == 0
    n_prompt_tiles = n_prompt // tm
    tiles_per_seq = seq // tm

    x = jnp.concatenate([x_prompt.reshape(n_prompt, d), x_sample.reshape(n_sample, d)], axis=0)

    pos = jnp.concatenate([jnp.arange(seq), past + (jnp.arange(tm) % dec_seq)])
    cos_r, sin_r = _rope_tables(pos, hk // RET_HEADS)
    cos_m, sin_m = _rope_tables(pos, MLA_ROPE)
    cos_m = jnp.tile(cos_m, (1, 2 * LANES // MLA_ROPE))
    sin_m = jnp.tile(jnp.concatenate([-sin_m, sin_m], axis=-1), (1, LANES // MLA_ROPE))

    q, k, v, gate = _ret_in(x, g_pre_mix[0][None], ret_w_in[0].astype(BF16), cos_r, sin_r,
                            n_prompt_tiles, tiles_per_seq, hk, hv)
    onorm_p, s_prompt = _ret_prompt(q, k, v, _decay_consts(RET_CHUNK, RET_CHUNK), batch, seq)
    seqs = 16 // dec_seq if dec_seq < 16 else 1
    onorm_s, s_sample = _ret_sample(q, k, v, state_ret[0], _decay_consts(dec_seq, seqs * dec_seq),
                                    n_prompt, dec_batch, dec_seq)
    h2 = _mix_ffn(x, onorm_p, onorm_s, gate, ret_g_out[0][None], ret_w_out[0].astype(BF16), g_pre_ffn[0][None],
                  ffn_w_gu[0].astype(BF16), ffn_w_down[0].astype(BF16))

    q_lora = mla_w_dq.shape[2]
    w_uq = mla_w_uq[0].reshape(q_lora, MLA_HEADS, MLA_NOPE + MLA_ROPE)
    w_uq_n = w_uq[:, :, :MLA_NOPE].reshape(q_lora, MLA_HEADS * MLA_NOPE).astype(BF16)
    w_uq_p = w_uq[:, :, MLA_NOPE:].reshape(q_lora, MLA_HEADS * MLA_ROPE).astype(BF16)
    w_dkv_c = mla_w_dkv[:, :kv_lora].astype(BF16)
    w_dkv_p = jnp.pad(mla_w_dkv[:, kv_lora:], ((0, 0), (0, LANES - MLA_ROPE))).astype(BF16)
    seg = jnp.arange(MLA_HEADS * MLA_ROPE) // MLA_ROPE
    block_diag = (seg[:, None] == seg[None, :]).astype(BF16)
    g_kn_t = jnp.tile(mla_g_kn, MLA_HEADS)[None]
    weights = [
        g_pre_mix[1][None], mla_g_kv_in[None], mla_w_dq[0].astype(BF16), mla_g_qa[0][None], w_uq_n, w_uq_p,
        jnp.tile(mla_g_qn[0], MLA_HEADS)[None], jnp.tile(mla_g_qp[0], MLA_HEADS)[None],
        w_dkv_c, w_dkv_p, mla_g_kv_a[None], jnp.pad(mla_g_kpe, (0, LANES - MLA_ROPE))[None],
        mla_w_uk.astype(BF16), mla_w_uv.astype(BF16), g_kn_t, block_diag,
    ]
    scale = float((MLA_NOPE + MLA_ROPE) ** -0.5)
    qn, qp, ckv, kpe, kn, vv = _mla_proj(h2, weights, cos_m, sin_m, n_prompt_tiles, tiles_per_seq, scale)
    o_attn_p = _prompt_attn(qn, qp, kn, kpe, vv, batch, seq)
    qa, qps = _absorb_q(qn, qp, g_kn_t, mla_w_uk.astype(BF16), n_prompt, n_sample)
    o_attn_s = _sample_attn(page_table, qa, qps, ckv, kpe, mla_w_uk.T.astype(BF16), mla_w_uv.astype(BF16),
                            cache_ckv, cache_kpe, n_prompt, dec_batch, dec_seq)

    w_r = jnp.pad(moe_w_router[0], ((0, 0), (0, LANES - N_EXPERTS))).astype(BF16)
    tri = (jnp.arange(tm)[:, None] >= jnp.arange(tm)[None, :]).astype(BF16)
    h3, xf, info, info_t, counts = _route(h2, o_attn_p, o_attn_s, mla_w_o[0].astype(BF16), g_pre_ffn[1][None],
                                          w_r, tri)

    te = EXPERT_TILE
    counts = counts[0, :N_EXPERTS].astype(jnp.int32)
    tiles_e = (counts + te - 1) // te
    tile_end = jnp.cumsum(tiles_e)
    offsets = (tile_end - tiles_e) * te
    n_tiles = (2 * n) // te + N_EXPERTS
    tile_ids = jnp.arange(n_tiles, dtype=jnp.int32)
    tile_valid = (tile_ids < tile_end[-1]).astype(jnp.int32)
    tile_expert = jnp.minimum(jnp.sum(tile_ids[:, None] >= tile_end[None, :], axis=1), N_EXPERTS - 1)
    last_expert = jnp.sum((tile_end[-1] - 1) >= tile_end).astype(jnp.int32)
    tile_expert = jnp.where(tile_valid == 1, tile_expert, last_expert).astype(jnp.int32)
    idx = info_t[0:2].astype(jnp.int32)
    dest = offsets[idx] + info_t[2:4].astype(jnp.int32)
    dest = dest.reshape(2, n // tm, tm).transpose(1, 0, 2).reshape(n // tm, 1, 2 * tm)

    xs = _scatter_rows(dest, xf, jnp.zeros((n_tiles * te, d), F32))
    ys = _experts(tile_expert, tile_valid, xs, moe_w_gu[0].astype(BF16), moe_w_down[0].astype(BF16))
    y = _combine(dest, h3, info, ys)

    return (
        y[:n_prompt].reshape(batch, seq, d),
        y[n_prompt:].reshape(dec_batch, dec_seq, d),
        s_prompt[None],
        s_sample[None],
        ckv[:n_prompt].reshape(batch, seq, kv_lora),
        kpe[:n_prompt].reshape(batch, seq, MLA_ROPE),
        ckv[n_prompt:].reshape(dec_batch, dec_seq, kv_lora),
        kpe[n_prompt:].reshape(dec_batch, dec_seq, MLA_ROPE),
    )
```

```python
import functools

import jax
import jax.numpy as jnp
from jax import lax
from jax.experimental import pallas as pl
from jax.experimental.pallas import tpu as pltpu

F32 = jnp.float32
BF16 = jnp.bfloat16
EPS = 1e-6
ROPE_BASE = 10000.0

RET_HEADS = 4
RET_CHUNK = 128
MLA_HEADS = 8
MLA_NOPE = 128
MLA_ROPE = 64
MLA_DV = 128
N_EXPERTS = 8
LANES = 128

TOKEN_TILE = 512
RET_ROWS = 512
ATTN_TILE = 512
ATTN_QBLOCK = 512
EXPERT_TILE = 1024
FF_CHUNK = 512
PAGES_PER_CHUNK = 8
SAMPLE_KEY_BLOCK = 256
MIB = 1024 * 1024


def _params(semantics, vmem_mib):
    return pltpu.CompilerParams(dimension_semantics=semantics, vmem_limit_bytes=vmem_mib * MIB)


def _resident(shape):
    return pl.BlockSpec(shape, lambda *_: (0,) * len(shape), pipeline_mode=pl.Buffered(1))


def _rms(x, g):
    ms = jnp.mean(x * x, axis=-1, keepdims=True)
    return (x * lax.rsqrt(ms + EPS)) * g


def _dot(a, b):
    return jnp.dot(a, b, preferred_element_type=F32)


def _dot_nt(a, b):
    return lax.dot_general(a, b, (((1,), (1,)), ((), ())), preferred_element_type=F32)


def _dot_tn(a, b):
    return lax.dot_general(a, b, (((0,), (0,)), ((), ())), preferred_element_type=F32)


def _silu(x):
    return x * jax.nn.sigmoid(x)


def _split_specs(tm, width, n_prompt_tiles):
    return [
        pl.BlockSpec((tm, width), lambda i: (jnp.minimum(i, n_prompt_tiles - 1), 0)),
        pl.BlockSpec((tm, width), lambda i: (jnp.maximum(i - n_prompt_tiles, 0), 0), pipeline_mode=pl.Buffered(1)),
    ]


def _pick(p_ref, s_ref, n_prompt_tiles):
    return jnp.where(pl.program_id(0) < n_prompt_tiles, p_ref[...], s_ref[...])


def _ret_in_kernel(xp_ref, xs_ref, g_ref, w_ref, cos_ref, sin_ref, q_ref, k_ref, v_ref, gate_ref,
                   *, hk, hv, dk, n_prompt_tiles):
    xn = _rms(_pick(xp_ref, xs_ref, n_prompt_tiles), g_ref[...]).astype(BF16)
    cos = cos_ref[...]
    sin = sin_ref[...]
    half = dk // 2

    def rope_store(dst, col, acc, scale):
        x1 = acc[:, :half]
        x2 = acc[:, half:]
        dst[:, col:col + half] = ((x1 * cos - x2 * sin) * scale).astype(BF16)
        dst[:, col + half:col + dk] = ((x1 * sin + x2 * cos) * scale).astype(BF16)

    for h in range(hk // dk):
        rope_store(q_ref, h * dk, _dot(xn, w_ref[:, h * dk:(h + 1) * dk]), 1.0)
    for h in range(hk // dk):
        rope_store(k_ref, h * dk, _dot(xn, w_ref[:, hk + h * dk:hk + (h + 1) * dk]), dk ** -0.5)
    step = 512
    for c in range(hv // step):
        v_ref[:, c * step:(c + 1) * step] = _dot(xn, w_ref[:, 2 * hk + c * step:2 * hk + (c + 1) * step]).astype(BF16)
    for c in range(hv // step):
        gate_ref[:, c * step:(c + 1) * step] = _dot(
            xn, w_ref[:, 2 * hk + hv + c * step:2 * hk + hv + (c + 1) * step]).astype(BF16)


def _ret_in(x_p, x_s, g, w, cos_tab, sin_tab, n_prompt_tiles, tiles_per_seq, hk, hv):
    d = x_p.shape[1]
    n = x_p.shape[0] + x_s.shape[0]
    tm = TOKEN_TILE
    dk = hk // RET_HEADS

    def tab_map(i):
        return (jnp.where(i < n_prompt_tiles, i % tiles_per_seq, tiles_per_seq), 0)

    return pl.pallas_call(
        functools.partial(_ret_in_kernel, hk=hk, hv=hv, dk=dk, n_prompt_tiles=n_prompt_tiles),
        grid=(n // tm,),
        in_specs=_split_specs(tm, d, n_prompt_tiles) + [
            _resident((1, d)),
            _resident(w.shape),
            pl.BlockSpec((tm, dk // 2), tab_map),
            pl.BlockSpec((tm, dk // 2), tab_map),
        ],
        out_specs=[
            pl.BlockSpec((tm, hk), lambda i: (i, 0)),
            pl.BlockSpec((tm, hk), lambda i: (i, 0)),
            pl.BlockSpec((tm, hv), lambda i: (i, 0)),
            pl.BlockSpec((tm, hv), lambda i: (i, 0)),
        ],
        out_shape=[
            jax.ShapeDtypeStruct((n, hk), BF16),
            jax.ShapeDtypeStruct((n, hk), BF16),
            jax.ShapeDtypeStruct((n, hv), BF16),
            jax.ShapeDtypeStruct((n, hv), BF16),
        ],
        compiler_params=_params(("parallel",), 48),
        name="ret_in",
    )(x_p, x_s, g, w, cos_tab, sin_tab)


def _group_norm(o):
    mu = jnp.mean(o, axis=-1, keepdims=True)
    var = jnp.mean(jnp.square(o - mu), axis=-1, keepdims=True)
    return (o - mu) * lax.rsqrt(var + EPS)


def _ret_prompt_kernel(q_ref, k_ref, v_ref, dmat_ref, dq_ref, dk_ref, gc_ref, o_ref, sfin_ref, state_ref,
                       *, dk, dv, chunk):
    j = pl.program_id(1)

    @pl.when(j == 0)
    def _():
        state_ref[...] = jnp.zeros_like(state_ref)

    def one_chunk(c, carry):
        r0 = pl.multiple_of(c * chunk, chunk)
        rows = pl.ds(r0, chunk)
        for h in range(RET_HEADS):
            q = q_ref[rows, h * dk:(h + 1) * dk]
            k = k_ref[rows, h * dk:(h + 1) * dk]
            v = v_ref[rows, h * dv:(h + 1) * dv]
            s = state_ref[h]
            inner = _dot_nt(q, k) * dmat_ref[h]
            o = _dot(inner.astype(BF16), v) + _dot(q, s.astype(BF16)) * dq_ref[h]
            kw = (k.astype(F32) * dk_ref[h]).astype(BF16)
            state_ref[h] = s * gc_ref[h] + _dot_tn(kw, v)
            o_ref[rows, h * dv:(h + 1) * dv] = _group_norm(o).astype(BF16)
        return carry

    lax.fori_loop(0, q_ref.shape[0] // chunk, one_chunk, 0)

    @pl.when(j == pl.num_programs(1) - 1)
    def _():
        sfin_ref[0] = state_ref[...]


def _ret_prompt(q, k, v, consts, batch, seq):
    hk = q.shape[1]
    hv = v.shape[1]
    dk, dv = hk // RET_HEADS, hv // RET_HEADS
    rb = RET_ROWS
    per_seq = seq // rb
    dmat, dq, dkk, gc = consts
    row = lambda b, j: (b * per_seq + j, 0)
    return pl.pallas_call(
        functools.partial(_ret_prompt_kernel, dk=dk, dv=dv, chunk=RET_CHUNK),
        grid=(batch, per_seq),
        in_specs=[
            pl.BlockSpec((rb, hk), row),
            pl.BlockSpec((rb, hk), row),
            pl.BlockSpec((rb, hv), row),
            _resident(dmat.shape),
            _resident(dq.shape),
            _resident(dkk.shape),
            _resident(gc.shape),
        ],
        out_specs=[
            pl.BlockSpec((rb, hv), row),
            pl.BlockSpec((1, RET_HEADS, dk, dv), lambda b, j: (b, 0, 0, 0)),
        ],
        out_shape=[
            jax.ShapeDtypeStruct((batch * seq, hv), BF16),
            jax.ShapeDtypeStruct((batch, RET_HEADS, dk, dv), F32),
        ],
        scratch_shapes=[pltpu.VMEM((RET_HEADS, dk, dv), F32)],
        compiler_params=_params(("parallel", "arbitrary"), 40),
        name="ret_prompt",
    )(q, k, v, dmat, dq, dkk, gc)


def _ret_sample_kernel(q_ref, k_ref, v_ref, s_ref, dmat_ref, dq_ref, dk_ref, gc_ref,
                       o_ref, snew_ref, *, dk, dv, dec_seq, seqs):
    rows = seqs * dec_seq
    row_seq = lax.broadcasted_iota(jnp.int32, (rows, 1), 0) // dec_seq
    for h in range(RET_HEADS):
        q = q_ref[:, h * dk:(h + 1) * dk]
        k = k_ref[:, h * dk:(h + 1) * dk]
        v = v_ref[:, h * dv:(h + 1) * dv]
        inner = _dot_nt(q, k) * dmat_ref[h]
        o = _dot(inner.astype(BF16), v)
        kw = k.astype(F32) * dk_ref[h]
        cross = jnp.zeros((rows, dv), F32)
        for s in range(seqs):
            st = s_ref[s, h]
            mine = row_seq == s
            cross = cross + jnp.where(mine, _dot(q, st.astype(BF16)), 0.0)
            kws = jnp.where(mine, kw, 0.0).astype(BF16)
            snew_ref[s, h] = st * gc_ref[h] + _dot_tn(kws, v)
        o = o + cross * dq_ref[h]
        o_ref[:, h * dv:(h + 1) * dv] = _group_norm(o).astype(BF16)


def _ret_sample(q, k, v, state, consts, n_prompt, dec_batch, dec_seq):
    hk = q.shape[1]
    hv = v.shape[1]
    dk, dv = hk // RET_HEADS, hv // RET_HEADS
    seqs = 16 // dec_seq if dec_seq < 16 else 1
    rows = seqs * dec_seq
    base = n_prompt // rows
    dmat, dq, dkk, gc = consts
    row = lambda i: (base + i, 0)
    return pl.pallas_call(
        functools.partial(_ret_sample_kernel, dk=dk, dv=dv, dec_seq=dec_seq, seqs=seqs),
        grid=(dec_batch // seqs,),
        in_specs=[
            pl.BlockSpec((rows, hk), row),
            pl.BlockSpec((rows, hk), row),
            pl.BlockSpec((rows, hv), row),
            pl.BlockSpec((seqs, RET_HEADS, dk, dv), lambda i: (i, 0, 0, 0)),
            _resident(dmat.shape),
            _resident(dq.shape),
            _resident(dkk.shape),
            _resident(gc.shape),
        ],
        out_specs=[
            pl.BlockSpec((rows, hv), lambda i: (i, 0)),
            pl.BlockSpec((seqs, RET_HEADS, dk, dv), lambda i: (i, 0, 0, 0)),
        ],
        out_shape=[
            jax.ShapeDtypeStruct((dec_batch * dec_seq, hv), BF16),
            jax.ShapeDtypeStruct(state.shape, F32),
        ],
        compiler_params=_params(("parallel",), 40),
        name="ret_sample",
    )(q, k, v, state, dmat, dq, dkk, gc)


def _decay_consts(chunk, rows):
    log_gamma = jnp.log1p(-jnp.exp2(-5.0 - jnp.arange(RET_HEADS, dtype=F32)))
    idx = jnp.arange(rows)
    pos = (idx % chunk).astype(F32)
    same = (idx[:, None] // chunk) == (idx[None, :] // chunk)
    diff = pos[:, None] - pos[None, :]
    lg = log_gamma[:, None, None]
    dmat = jnp.where(same & (diff >= 0), jnp.exp(lg * jnp.maximum(diff, 0.0)), 0.0)
    dq = jnp.exp(log_gamma[:, None] * (pos + 1.0))[..., None]
    dk = jnp.exp(log_gamma[:, None] * (chunk - 1.0 - pos))[..., None]
    gc = jnp.exp(log_gamma * chunk)[:, None, None]
    return dmat.astype(F32), dq.astype(F32), dk.astype(F32), gc.astype(F32)


def _mix_ffn_kernel(xp_ref, xs_ref, op_ref, os_ref, gate_ref, gout_ref, wout_ref, gffn_ref, wgu_ref, wdown_ref,
                    h_ref, act_ref, *, d_ff, chunk, n_prompt_tiles):
    o = _pick(op_ref, os_ref, n_prompt_tiles).astype(F32) * gout_ref[...]
    z = (_silu(gate_ref[...].astype(F32)) * o).astype(BF16)
    h1 = _pick(xp_ref, xs_ref, n_prompt_tiles) + _dot(z, wout_ref[...])
    xf = _rms(h1, gffn_ref[...]).astype(BF16)
    for c in range(d_ff // chunk):
        g = _dot(xf, wgu_ref[:, c * chunk:(c + 1) * chunk])
        u = _dot(xf, wgu_ref[:, d_ff + c * chunk:d_ff + (c + 1) * chunk])
        act_ref[:, c * chunk:(c + 1) * chunk] = (_silu(g) * u).astype(BF16)
    h_ref[...] = h1 + _dot(act_ref[...], wdown_ref[...])


def _mix_ffn(x_p, x_s, onorm_p, onorm_s, gate, g_out, w_out, g_ffn, w_gu, w_down):
    d = x_p.shape[1]
    n = x_p.shape[0] + x_s.shape[0]
    hv = onorm_p.shape[1]
    d_ff = w_down.shape[0]
    tm = TOKEN_TILE
    npt = onorm_p.shape[0] // tm
    return pl.pallas_call(
        functools.partial(_mix_ffn_kernel, d_ff=d_ff, chunk=256, n_prompt_tiles=npt),
        grid=(n // tm,),
        in_specs=_split_specs(tm, d, npt) + _split_specs(tm, hv, npt) + [
            pl.BlockSpec((tm, hv), lambda i: (i, 0)),
            _resident((1, hv)),
            _resident(w_out.shape),
            _resident((1, d)),
            _resident(w_gu.shape),
            _resident(w_down.shape),
        ],
        out_specs=pl.BlockSpec((tm, d), lambda i: (i, 0)),
        out_shape=jax.ShapeDtypeStruct((n, d), F32),
        scratch_shapes=[pltpu.VMEM((tm, d_ff), BF16)],
        compiler_params=_params(("parallel",), 56),
        name="mix_ffn",
    )(x_p, x_s, onorm_p, onorm_s, gate, g_out, w_out, g_ffn, w_gu, w_down)


def _rope_lanes(x, cos, sin_signed, half):
    width = x.shape[-1]
    lane = lax.broadcasted_iota(jnp.int32, x.shape, 1)
    first = (lane % (2 * half)) < half
    rot = jnp.where(first, pltpu.roll(x, width - half, axis=1), pltpu.roll(x, half, axis=1))
    return x * cos + rot * sin_signed


def _unit_rms(x, width):
    ms = jnp.sum(x * x, axis=-1, keepdims=True) * (1.0 / width)
    return x * lax.rsqrt(ms + EPS)


HEAD_PAD = 256
V_PAD = 144


def _mla_proj_kernel(h_ref, gmix_ref, gkv_ref, wdq_ref, gqa_ref, wuq_ref, gqn_ref, gqp_ref,
                     wdkvc_ref, wdkvp_ref, gkva_ref, gkpe_ref, wuk_ref, wuv_ref, gkn_ref,
                     cos_ref, sin_ref,
                     qcat_ref, kcat_ref, vt_ref, ckv_ref, kpe_ref, *, scale):
    h = h_ref[...]
    r = lax.rsqrt(jnp.mean(h * h, axis=-1, keepdims=True) + EPS)
    xn = ((h * r) * gmix_ref[...]).astype(BF16)
    xk = ((h * r) * gkv_ref[...]).astype(BF16)
    cos = cos_ref[...]
    sin = sin_ref[...]
    half = MLA_ROPE // 2

    ckv = _rms(_dot(xk, wdkvc_ref[...]), gkva_ref[...])
    ckv_ref[...] = ckv
    ap = _dot(xk, wdkvp_ref[...])
    kpe = _rope_lanes(_unit_rms(ap, MLA_ROPE) * gkpe_ref[...], cos, sin, half)
    kpe_ref[...] = kpe[:, :MLA_ROPE]
    kpe_b = kpe.astype(BF16)

    cq = _rms(_dot(xn, wdq_ref[...]), gqa_ref[...]).astype(BF16)
    q = _dot(cq, wuq_ref[...])
    cb = ckv.astype(BF16)
    kn = _dot(cb, wuk_ref[...])
    for hd in range(MLA_HEADS):
        c0 = hd * HEAD_PAD
        qn = _unit_rms(q[:, c0:c0 + MLA_NOPE], MLA_NOPE) * gqn_ref[...]
        qp = _unit_rms(q[:, c0 + MLA_NOPE:c0 + HEAD_PAD], MLA_ROPE) * gqp_ref[...]
        qp = _rope_lanes(qp, cos, sin, half)
        qcat_ref[:, c0:c0 + MLA_NOPE] = (qn * scale).astype(BF16)
        qcat_ref[:, c0 + MLA_NOPE:c0 + HEAD_PAD] = (qp * scale).astype(BF16)
        kh = _unit_rms(kn[:, hd * MLA_NOPE:(hd + 1) * MLA_NOPE], MLA_NOPE) * gkn_ref[...]
        kcat_ref[:, c0:c0 + MLA_NOPE] = kh.astype(BF16)
        kcat_ref[:, c0 + MLA_NOPE:c0 + HEAD_PAD] = kpe_b
    vt = _dot(cb, wuv_ref[...]).T
    ones = jnp.ones((V_PAD - MLA_DV, vt.shape[1]), BF16)
    for hd in range(MLA_HEADS):
        vt_ref[hd * V_PAD:hd * V_PAD + MLA_DV, :] = vt[hd * MLA_DV:(hd + 1) * MLA_DV, :].astype(BF16)
        vt_ref[hd * V_PAD + MLA_DV:(hd + 1) * V_PAD, :] = ones


def _mla_proj(h, weights, cos_tab, sin_tab, n_prompt_tiles, tiles_per_seq, scale):
    n, d = h.shape
    tm = TOKEN_TILE
    hc = MLA_HEADS * HEAD_PAD
    hn = MLA_HEADS * V_PAD
    kv_lora = weights[7].shape[1]

    def tab_map(i):
        return (jnp.where(i < n_prompt_tiles, i % tiles_per_seq, tiles_per_seq), 0)

    row = lambda i: (i, 0)
    in_specs = [pl.BlockSpec((tm, d), row)] + [_resident(w.shape) for w in weights]
    in_specs += [pl.BlockSpec((tm, LANES), tab_map), pl.BlockSpec((tm, LANES), tab_map)]
    return pl.pallas_call(
        functools.partial(_mla_proj_kernel, scale=scale),
        grid=(n // tm,),
        in_specs=in_specs,
        out_specs=[
            pl.BlockSpec((tm, hc), row),
            pl.BlockSpec((tm, hc), row),
            pl.BlockSpec((hn, tm), lambda i: (0, i)),
            pl.BlockSpec((tm, kv_lora), row),
            pl.BlockSpec((tm, MLA_ROPE), row),
        ],
        out_shape=[
            jax.ShapeDtypeStruct((n, hc), BF16),
            jax.ShapeDtypeStruct((n, hc), BF16),
            jax.ShapeDtypeStruct((hn, n), BF16),
            jax.ShapeDtypeStruct((n, kv_lora), F32),
            jax.ShapeDtypeStruct((n, MLA_ROPE), F32),
        ],
        compiler_params=_params(("parallel",), 48),
        name="mla_proj",
    )(h, *weights, cos_tab, sin_tab)


def _prompt_attn_kernel(q_ref, k_ref, vt_ref, o_ref, *scratch, tile, qblock):
    m_refs = scratch[:MLA_HEADS]
    acc_refs = scratch[MLA_HEADS:]
    qi = pl.program_id(1)
    ki = pl.program_id(2)

    @pl.when(ki == 0)
    def _():
        for h in range(MLA_HEADS):
            m_refs[h][...] = jnp.full_like(m_refs[h], -jnp.inf)
            acc_refs[h][...] = jnp.zeros_like(acc_refs[h])

    def key_tile(diagonal):
        if diagonal:
            key = lax.broadcasted_iota(jnp.int32, (tile, qblock), 0)
            qry = lax.broadcasted_iota(jnp.int32, (tile, qblock), 1)
        for h in range(MLA_HEADS):
            cols = slice(h * HEAD_PAD, (h + 1) * HEAD_PAD)
            vt = vt_ref[h * V_PAD:(h + 1) * V_PAD, :]
            for j in range(tile // qblock):
                lanes = slice(j * qblock, (j + 1) * qblock)
                s = _dot_nt(k_ref[:, cols], q_ref[lanes, cols])
                if diagonal:
                    s = jnp.where(key <= qry + j * qblock, s, -jnp.inf)
                m_old = m_refs[h][:, lanes]
                m_new = jnp.maximum(m_old, jnp.max(s, axis=0, keepdims=True))
                alpha = jnp.exp2(m_old - m_new)
                p = jnp.exp2(s - m_new)
                m_refs[h][:, lanes] = m_new
                acc_refs[h][:, lanes] = alpha * acc_refs[h][:, lanes] + _dot(vt, p.astype(BF16))

    @pl.when(ki < qi)
    def _():
        key_tile(False)

    @pl.when(ki == qi)
    def _():
        key_tile(True)
        outs = []
        for h in range(MLA_HEADS):
            blk = acc_refs[h][...]
            outs.append(blk[:MLA_DV] / blk[MLA_DV:MLA_DV + 1])
        o_ref[...] = jnp.concatenate(outs, axis=0).T


def _prompt_attn(qcat, kcat, vt, batch, seq):
    t = ATTN_TILE
    per_seq = seq // t
    hc = qcat.shape[1]
    hn = MLA_HEADS * MLA_DV
    hvp = vt.shape[0]
    qrow = lambda b, qi, ki: (b * per_seq + qi, 0)
    krow = lambda b, qi, ki: (b * per_seq + jnp.minimum(ki, qi), 0)
    vcol = lambda b, qi, ki: (0, b * per_seq + jnp.minimum(ki, qi))
    return pl.pallas_call(
        functools.partial(_prompt_attn_kernel, tile=t, qblock=ATTN_QBLOCK),
        grid=(batch, per_seq, per_seq),
        in_specs=[
            pl.BlockSpec((t, hc), qrow),
            pl.BlockSpec((t, hc), krow),
            pl.BlockSpec((hvp, t), vcol),
        ],
        out_specs=pl.BlockSpec((t, hn), qrow),
        out_shape=jax.ShapeDtypeStruct((batch * seq, hn), F32),
        scratch_shapes=([pltpu.VMEM((1, t), F32)] * MLA_HEADS + [pltpu.VMEM((V_PAD, t), F32)] * MLA_HEADS),
        compiler_params=_params(("parallel", "parallel", "arbitrary"), 40),
        name="prompt_attn",
    )(qcat, kcat, vt)


def _absorb_q_kernel(qcat_ref, gkn_ref, wuk_ref, qa_ref, qps_ref):
    lora = wuk_ref.shape[0]
    for h in range(MLA_HEADS):
        c0 = h * HEAD_PAD
        qg = (qcat_ref[:, c0:c0 + MLA_NOPE].astype(F32) * gkn_ref[...]).astype(BF16)
        qa_ref[:, h * lora:(h + 1) * lora] = _dot_nt(qg, wuk_ref[:, h * MLA_NOPE:(h + 1) * MLA_NOPE])
        qps_ref[:, h * LANES:(h + 1) * LANES] = qcat_ref[:, c0 + MLA_NOPE:c0 + HEAD_PAD].astype(F32)


def _absorb_q(qcat, gkn, wuk, n_prompt, n_sample):
    hc = qcat.shape[1]
    lora = wuk.shape[0]
    tm = min(256, n_sample)
    base = n_prompt // tm
    return pl.pallas_call(
        _absorb_q_kernel,
        grid=(n_sample // tm,),
        in_specs=[
            pl.BlockSpec((tm, hc), lambda i: (base + i, 0)),
            _resident(gkn.shape),
            _resident(wuk.shape),
        ],
        out_specs=[
            pl.BlockSpec((tm, MLA_HEADS * lora), lambda i: (i, 0)),
            pl.BlockSpec((tm, MLA_HEADS * LANES), lambda i: (i, 0)),
        ],
        out_shape=[
            jax.ShapeDtypeStruct((n_sample, MLA_HEADS * lora), F32),
            jax.ShapeDtypeStruct((n_sample, MLA_HEADS * LANES), F32),
        ],
        compiler_params=_params(("parallel",), 32),
        name="absorb_q",
    )(qcat, gkn, wuk)


def _sample_attn_kernel(pt_ref, qa_ref, qp_ref, cnew_ref, pnew_ref, wukt_ref, wuv_ref, cache_c_ref, cache_pt_ref,
                        o_ref, cbuf, pbuf, sem, m_ref, l_ref, acc_ref, lhs_s, qa_s, qp_s, cnew_s, pnew_s,
                        cbb0, cbb1, s_scr0, s_scr1, *, n_pages, ppc, page, dec_seq, lora, key_block):
    b = pl.program_id(0)
    nb = pl.num_programs(0)
    n_chunks = n_pages // ppc
    hq = MLA_HEADS * dec_seq
    n_proj = MLA_HEADS * MLA_NOPE

    def copies(seq, chunk, slot):
        out = []
        for p in range(ppc):
            pg = pt_ref[seq, chunk * ppc + p]
            out.append(pltpu.make_async_copy(cache_c_ref.at[pg], cbuf.at[slot, pl.ds(p * page, page)], sem.at[0, slot]))
            out.append(pltpu.make_async_copy(cache_pt_ref.at[pg], pbuf.at[slot, :, pl.ds(p * page, page)],
                                             sem.at[1, slot]))
        return out

    @pl.when(b == 0)
    def _():
        for cp in copies(0, 0, 0):
            cp.start()
        lhs_s[0:n_proj, :] = wukt_ref[...]

    for h in range(MLA_HEADS):
        qa_s[h * dec_seq:(h + 1) * dec_seq, :] = qa_ref[:, h * lora:(h + 1) * lora]
        qp_s[h * dec_seq:(h + 1) * dec_seq, :] = qp_ref[:, h * LANES:(h + 1) * LANES]
    lhs_s[n_proj:n_proj + hq, :] = qa_s[...].astype(BF16)
    qp = qp_s[...][:, :MLA_ROPE].astype(BF16)
    m_ref[...] = jnp.full_like(m_ref, -jnp.inf)
    l_ref[...] = jnp.zeros_like(l_ref)
    acc_ref[...] = jnp.zeros_like(acc_ref)

    def latent_scores(cb):
        t = cb.shape[0]
        kb = min(key_block, t)
        blocks = []
        for j in range(t // kb):
            kt = _dot_nt(lhs_s[...], cb[j * kb:(j + 1) * kb, :])
            parts = []
            for h in range(MLA_HEADS):
                kh = kt[h * MLA_NOPE:(h + 1) * MLA_NOPE, :]
                ms = jnp.sum(kh * kh, axis=0, keepdims=True) * (1.0 / MLA_NOPE)
                parts.append(kt[n_proj + h * dec_seq:n_proj + (h + 1) * dec_seq, :] * lax.rsqrt(ms + EPS))
            blocks.append(jnp.concatenate(parts, axis=0))
        return blocks[0] if len(blocks) == 1 else jnp.concatenate(blocks, axis=1)

    cbb = (cbb0, cbb1)
    s_scr = (s_scr0, s_scr1)

    def project(slot, par):
        cb = cbuf[slot].astype(BF16)
        cbb[par][...] = cb
        s_scr[par][...] = latent_scores(cb) + _dot(qp, pbuf[slot].astype(BF16))

    def absorb(s, cb):
        m_old = m_ref[...]
        m_new = jnp.maximum(m_old, jnp.max(s, axis=-1, keepdims=True))
        alpha = jnp.exp2(m_old - m_new)
        p = jnp.exp2(s - m_new)
        l_ref[...] = alpha * l_ref[...] + jnp.sum(p, axis=-1, keepdims=True)
        m_ref[...] = m_new
        acc_ref[...] = alpha * acc_ref[...] + _dot(p.astype(BF16), cb)

    def fetch_next(c, slot):
        if c + 1 < n_chunks:
            for cp in copies(b, c + 1, 1 - slot):
                cp.start()
        else:
            @pl.when(b + 1 < nb)
            def _():
                for cp in copies(b + 1, 0, 1 - slot):
                    cp.start()

        for cp in copies(b, c, slot):
            cp.wait()

    for c in range(n_chunks):
        slot = (b * n_chunks + c) % 2 if n_chunks % 2 else c % 2
        fetch_next(c, slot)
        project(slot, c % 2)
        if c:
            absorb(s_scr[(c - 1) % 2][...], cbb[(c - 1) % 2][...])

    cnew_s[...] = jnp.zeros_like(cnew_s)
    pnew_s[...] = jnp.zeros_like(pnew_s)
    cnew_s[0:dec_seq, :] = cnew_ref[...]
    pnew_s[0:dec_seq, :] = pnew_ref[...]
    cnew = cnew_s[...].astype(BF16)
    row_tok = lax.broadcasted_iota(jnp.int32, (hq, LANES), 0) % dec_seq
    col_tok = lax.broadcasted_iota(jnp.int32, (hq, LANES), 1)
    s_new = latent_scores(cnew) + _dot_nt(qp, pnew_s[...].astype(BF16))
    s_new = jnp.where(col_tok <= row_tok, s_new, -jnp.inf)
    last = (n_chunks - 1) % 2
    absorb(jnp.concatenate([s_scr[last][...], s_new], axis=1), jnp.concatenate([cbb[last][...], cnew], axis=0))

    lat = (acc_ref[...] / l_ref[...]).astype(BF16)
    full = _dot(lat, wuv_ref[...])
    for h in range(MLA_HEADS):
        cols = slice(h * MLA_DV, (h + 1) * MLA_DV)
        o_ref[:, cols] = full[h * dec_seq:(h + 1) * dec_seq, cols]


def _sample_attn(page_table, qa, qp, ckv, kpe, wukt, wuv, cache_c, cache_pt, n_prompt, dec_batch, dec_seq):
    n_pages = page_table.shape[1]
    page = cache_c.shape[1]
    lora = cache_c.shape[2]
    ppc = min(PAGES_PER_CHUNK, n_pages)
    hn = MLA_HEADS * MLA_DV
    hq = MLA_HEADS * dec_seq
    base = n_prompt // dec_seq
    srow = lambda b, pt: (b, 0)
    nrow = lambda b, pt: (base + b, 0)
    grid_spec = pltpu.PrefetchScalarGridSpec(
        num_scalar_prefetch=1,
        grid=(dec_batch,),
        in_specs=[
            pl.BlockSpec((dec_seq, MLA_HEADS * lora), srow),
            pl.BlockSpec((dec_seq, MLA_HEADS * LANES), srow),
            pl.BlockSpec((dec_seq, lora), nrow),
            pl.BlockSpec((dec_seq, MLA_ROPE), nrow),
            pl.BlockSpec(wukt.shape, lambda b, pt: (0, 0), pipeline_mode=pl.Buffered(1)),
            pl.BlockSpec(wuv.shape, lambda b, pt: (0, 0), pipeline_mode=pl.Buffered(1)),
            pl.BlockSpec(memory_space=pl.ANY),
            pl.BlockSpec(memory_space=pl.ANY),
        ],
        out_specs=pl.BlockSpec((dec_seq, hn), srow),
        scratch_shapes=[
            pltpu.VMEM((2, ppc * page, lora), F32),
            pltpu.VMEM((2, MLA_ROPE, ppc * page), F32),
            pltpu.SemaphoreType.DMA((2, 2)),
            pltpu.VMEM((hq, 1), F32),
            pltpu.VMEM((hq, 1), F32),
            pltpu.VMEM((hq, lora), F32),
            pltpu.VMEM((MLA_HEADS * MLA_NOPE + hq, lora), BF16),
            pltpu.VMEM((hq, lora), F32),
            pltpu.VMEM((hq, LANES), F32),
            pltpu.VMEM((LANES, lora), F32),
            pltpu.VMEM((LANES, MLA_ROPE), F32),
            pltpu.VMEM((ppc * page, lora), BF16),
            pltpu.VMEM((ppc * page, lora), BF16),
            pltpu.VMEM((hq, ppc * page), F32),
            pltpu.VMEM((hq, ppc * page), F32),
        ],
    )
    return pl.pallas_call(
        functools.partial(_sample_attn_kernel, n_pages=n_pages, ppc=ppc, page=page, dec_seq=dec_seq, lora=lora,
                          key_block=SAMPLE_KEY_BLOCK),
        grid_spec=grid_spec,
        out_shape=jax.ShapeDtypeStruct((dec_batch * dec_seq, hn), F32),
        compiler_params=_params(("arbitrary",), 40),
        name="sample_attn",
    )(page_table, qa, qp, ckv, kpe, wukt, wuv, cache_c, cache_pt)


def _route_kernel(h_ref, op_ref, os_ref, wo_ref, gffn_ref, wr_ref, tri_ref, h3_ref, xf_ref, info_ref, infot_ref,
                  cnt_ref, carry_ref, *, n_prompt_tiles):
    i = pl.program_id(0)

    @pl.when(i == 0)
    def _():
        carry_ref[...] = jnp.zeros_like(carry_ref)

    o = jnp.where(i < n_prompt_tiles, op_ref[...], os_ref[...])
    h3 = h_ref[...] + _dot(o.astype(BF16), wo_ref[...])
    h3_ref[...] = h3
    xf = _rms(h3, gffn_ref[...])
    xf_ref[...] = xf
    logits = _dot(xf.astype(BF16), wr_ref[...])
    lane = lax.broadcasted_iota(jnp.int32, logits.shape, 1).astype(F32)
    lg = jnp.where(lane < N_EXPERTS, logits, -jnp.inf)
    v1 = jnp.max(lg, axis=-1, keepdims=True)
    i1 = jnp.min(jnp.where(lg == v1, lane, float(LANES)), axis=-1, keepdims=True)
    lg2 = jnp.where(lane == i1, -jnp.inf, lg)
    v2 = jnp.max(lg2, axis=-1, keepdims=True)
    i2 = jnp.min(jnp.where(lg2 == v2, lane, float(LANES)), axis=-1, keepdims=True)
    e = jnp.exp(v2 - v1)
    g1 = 1.0 / (1.0 + e)
    g2 = e / (1.0 + e)
    oh1 = lane == i1
    oh2 = lane == i2
    chosen = jnp.where(oh1 | oh2, 1.0, 0.0)
    cum = _dot(tri_ref[...], chosen.astype(BF16))
    before = cum - chosen + carry_ref[...]
    pos1 = jnp.sum(jnp.where(oh1, before, 0.0), axis=-1, keepdims=True)
    pos2 = jnp.sum(jnp.where(oh2, before, 0.0), axis=-1, keepdims=True)
    carry_ref[...] = carry_ref[...] + cum[cum.shape[0] - 1:, :]
    cnt_ref[...] = carry_ref[...]
    info = jnp.where(lane == 0, i1, 0.0)
    info = jnp.where(lane == 1, i2, info)
    info = jnp.where(lane == 2, pos1, info)
    info = jnp.where(lane == 3, pos2, info)
    info = jnp.where(lane == 4, g1, info)
    info = jnp.where(lane == 5, g2, info)
    info_ref[...] = info
    infot_ref[...] = info.T[:8, :]


def _route(h2, o_attn_p, o_attn_s, w_o, g_ffn, w_r, tri):
    n, d = h2.shape
    tm = TOKEN_TILE
    npt = o_attn_p.shape[0] // tm
    hn = o_attn_p.shape[1]
    row = lambda i: (i, 0)
    return pl.pallas_call(
        functools.partial(_route_kernel, n_prompt_tiles=npt),
        grid=(n // tm,),
        in_specs=[
            pl.BlockSpec((tm, d), row),
            pl.BlockSpec((tm, hn), lambda i: (jnp.minimum(i, npt - 1), 0)),
            pl.BlockSpec((tm, hn), lambda i: (jnp.maximum(i - npt, 0), 0)),
            _resident(w_o.shape),
            _resident((1, d)),
            _resident(w_r.shape),
            _resident(tri.shape),
        ],
        out_specs=[
            pl.BlockSpec((tm, d), row),
            pl.BlockSpec((tm, d), row),
            pl.BlockSpec((tm, LANES), row),
            pl.BlockSpec((8, tm), lambda i: (0, i)),
            pl.BlockSpec((1, LANES), lambda i: (0, 0)),
        ],
        out_shape=[
            jax.ShapeDtypeStruct((n, d), F32),
            jax.ShapeDtypeStruct((n, d), F32),
            jax.ShapeDtypeStruct((n, LANES), F32),
            jax.ShapeDtypeStruct((8, n), F32),
            jax.ShapeDtypeStruct((1, LANES), F32),
        ],
        scratch_shapes=[pltpu.VMEM((1, LANES), F32)],
        compiler_params=_params(("arbitrary",), 40),
        name="route",
    )(h2, o_attn_p, o_attn_s, w_o, g_ffn, w_r, tri)


def _scatter_kernel(dest_ref, x_ref, xs_in_ref, xs_ref, sem, *, tm):
    del xs_in_ref

    def issue(r, carry):
        for k in range(2):
            d = dest_ref[0, 0, k * tm + r]
            pltpu.make_async_copy(x_ref.at[pl.ds(r, 1)], xs_ref.at[pl.ds(d, 1)], sem).start(priority=k)
        return carry

    lax.fori_loop(0, tm, issue, 0, unroll=8)

    def drain(r, carry):
        for k in range(2):
            pltpu.make_async_copy(x_ref.at[pl.ds(r, 1)], xs_ref.at[pl.ds(0, 1)], sem).wait()
        return carry

    lax.fori_loop(0, tm, drain, 0, unroll=8)


def _scatter_rows(dest, xf, xs0):
    n, d = xf.shape
    tm = TOKEN_TILE
    return pl.pallas_call(
        functools.partial(_scatter_kernel, tm=tm),
        grid=(n // tm,),
        in_specs=[
            pl.BlockSpec((1, 1, 2 * tm), lambda i: (i, 0, 0), memory_space=pltpu.SMEM),
            pl.BlockSpec((tm, d), lambda i: (i, 0)),
            pl.BlockSpec(memory_space=pl.ANY),
        ],
        out_specs=pl.BlockSpec(memory_space=pl.ANY),
        out_shape=jax.ShapeDtypeStruct(xs0.shape, xs0.dtype),
        scratch_shapes=[pltpu.SemaphoreType.DMA(())],
        input_output_aliases={2: 0},
        compiler_params=_params(("arbitrary",), 32),
        name="scatter_rows",
    )(dest, xf, xs0)


def _experts_kernel(te_ref, tv_ref, x_ref, wg_ref, wu_ref, wd_ref, o_ref, acc_ref, xb_ref):
    t = pl.program_id(0)
    c = pl.program_id(1)

    @pl.when(tv_ref[t] == 1)
    def _():
        @pl.when(c == 0)
        def _():
            xb_ref[...] = x_ref[...].astype(BF16)
            acc_ref[...] = jnp.zeros_like(acc_ref)

        xb = xb_ref[...]
        act = (_silu(_dot(xb, wg_ref[0])) * _dot(xb, wu_ref[0])).astype(BF16)
        acc_ref[...] += _dot(act, wd_ref[0])

        @pl.when(c == pl.num_programs(1) - 1)
        def _():
            o_ref[...] = acc_ref[...]

    @pl.when((tv_ref[t] == 0) & (c == 0))
    def _():
        o_ref[...] = jnp.zeros_like(o_ref)


def _experts(tile_expert, tile_valid, xs, w_gu, w_down):
    rows, d = xs.shape
    tm = EXPERT_TILE
    fc = FF_CHUNK
    d_ff = w_down.shape[1]
    n_c = d_ff // fc
    n_tiles = rows // tm

    def ceff(t, c, tv):
        return jnp.where(tv[t] == 1, c, n_c - 1)

    grid_spec = pltpu.PrefetchScalarGridSpec(
        num_scalar_prefetch=2,
        grid=(n_tiles, n_c),
        in_specs=[
            pl.BlockSpec((tm, d), lambda t, c, te, tv: (t, 0)),
            pl.BlockSpec((1, d, fc), lambda t, c, te, tv: (te[t], 0, ceff(t, c, tv))),
            pl.BlockSpec((1, d, fc), lambda t, c, te, tv: (te[t], 0, n_c + ceff(t, c, tv))),
            pl.BlockSpec((1, fc, d), lambda t, c, te, tv: (te[t], ceff(t, c, tv), 0)),
        ],
        out_specs=pl.BlockSpec((tm, d), lambda t, c, te, tv: (t, 0)),
        scratch_shapes=[pltpu.VMEM((tm, d), F32), pltpu.VMEM((tm, d), BF16)],
    )
    return pl.pallas_call(
        _experts_kernel,
        grid_spec=grid_spec,
        out_shape=jax.ShapeDtypeStruct((rows, d), F32),
        compiler_params=_params(("arbitrary", "arbitrary"), 52),
        name="experts",
    )(tile_expert, tile_valid, xs, w_gu, w_gu, w_down)


def _combine_kernel(dest_ref, h_ref, info_ref, ys_ref, yp_ref, ysm_ref, buf, sem, *, tm, n_prompt_tiles):
    def issue(r, carry):
        for k in range(2):
            d = dest_ref[0, 0, k * tm + r]
            pltpu.make_async_copy(ys_ref.at[pl.ds(d, 1)], buf.at[k, pl.ds(r, 1)], sem).start(priority=k)
        return carry

    lax.fori_loop(0, tm, issue, 0, unroll=8)

    def drain(r, carry):
        for k in range(2):
            pltpu.make_async_copy(ys_ref.at[pl.ds(0, 1)], buf.at[k, pl.ds(r, 1)], sem).wait()
        return carry

    lax.fori_loop(0, tm, drain, 0, unroll=8)
    info = info_ref[...]
    y = h_ref[...] + info[:, 4:5] * buf[0] + info[:, 5:6] * buf[1]

    @pl.when(pl.program_id(0) < n_prompt_tiles)
    def _():
        yp_ref[...] = y

    @pl.when(pl.program_id(0) >= n_prompt_tiles)
    def _():
        ysm_ref[...] = y


def _combine(dest, h3, info, ys, n_prompt):
    n, d = h3.shape
    tm = TOKEN_TILE
    npt = n_prompt // tm
    return pl.pallas_call(
        functools.partial(_combine_kernel, tm=tm, n_prompt_tiles=npt),
        grid=(n // tm,),
        in_specs=[
            pl.BlockSpec((1, 1, 2 * tm), lambda i: (i, 0, 0), memory_space=pltpu.SMEM),
            pl.BlockSpec((tm, d), lambda i: (i, 0)),
            pl.BlockSpec((tm, LANES), lambda i: (i, 0)),
            pl.BlockSpec(memory_space=pl.ANY),
        ],
        out_specs=[
            pl.BlockSpec((tm, d), lambda i: (jnp.minimum(i, npt - 1), 0)),
            pl.BlockSpec((tm, d), lambda i: (jnp.maximum(i - npt, 0), 0)),
        ],
        out_shape=[
            jax.ShapeDtypeStruct((n_prompt, d), F32),
            jax.ShapeDtypeStruct((n - n_prompt, d), F32),
        ],
        scratch_shapes=[pltpu.VMEM((2, tm, d), F32), pltpu.SemaphoreType.DMA(())],
        compiler_params=_params(("arbitrary",), 32),
        name="combine",
    )(dest, h3, info, ys)


def _rope_tables(pos, dim):
    inv = ROPE_BASE ** (-jnp.arange(0, dim, 2, dtype=F32) / dim)
    ang = pos.astype(F32)[:, None] * inv[None, :]
    return jnp.cos(ang), jnp.sin(ang)


def kernel(x_prompt, x_sample, state_ret, cache_ckv, cache_kpe, page_table, g_pre_mix, g_pre_ffn, ret_w_in, ret_g_out, ret_w_out, mla_g_kv_in, mla_w_dkv, mla_g_kv_a, mla_g_kpe, mla_w_uk, mla_w_uv, mla_g_kn, mla_w_dq, mla_g_qa, mla_w_uq, mla_g_qn, mla_g_qp, mla_w_o, ffn_w_gu, ffn_w_down, moe_w_router, moe_w_gu, moe_w_down):
    batch, seq, d = x_prompt.shape
    dec_batch, dec_seq, _ = x_sample.shape
    n_prompt = batch * seq
    n_sample = dec_batch * dec_seq
    n = n_prompt + n_sample
    tm = TOKEN_TILE
    past = page_table.shape[1] * cache_ckv.shape[1]
    hv = ret_g_out.shape[1]
    hk = (ret_w_in.shape[2] - 2 * hv) // 2
    kv_lora = mla_g_kv_a.shape[0]
    assert seq % tm == 0 and n_sample % tm == 0 and seq % ATTN_TILE == 0 and seq % RET_ROWS == 0
    n_prompt_tiles = n_prompt // tm
    tiles_per_seq = seq // tm

    x_p = x_prompt.reshape(n_prompt, d)
    x_s = x_sample.reshape(n_sample, d)

    pos = jnp.concatenate([jnp.arange(seq), past + (jnp.arange(tm) % dec_seq)])
    cos_r, sin_r = _rope_tables(pos, hk // RET_HEADS)
    cos_m, sin_m = _rope_tables(pos, MLA_ROPE)
    cos_m = jnp.tile(cos_m, (1, 2 * LANES // MLA_ROPE))
    sin_m = jnp.tile(jnp.concatenate([-sin_m, sin_m], axis=-1), (1, LANES // MLA_ROPE))

    q, k, v, gate = _ret_in(x_p, x_s, g_pre_mix[0][None], ret_w_in[0].astype(BF16), cos_r, sin_r,
                            n_prompt_tiles, tiles_per_seq, hk, hv)
    onorm_p, s_prompt = _ret_prompt(q, k, v, _decay_consts(RET_CHUNK, RET_CHUNK), batch, seq)
    seqs = 16 // dec_seq if dec_seq < 16 else 1
    onorm_s, s_sample = _ret_sample(q, k, v, state_ret[0], _decay_consts(dec_seq, seqs * dec_seq),
                                    n_prompt, dec_batch, dec_seq)
    h2 = _mix_ffn(x_p, x_s, onorm_p, onorm_s, gate, ret_g_out[0][None], ret_w_out[0].astype(BF16),
                  g_pre_ffn[0][None], ffn_w_gu[0].astype(BF16), ffn_w_down[0].astype(BF16))

    q_lora = mla_w_dq.shape[2]
    head_dim = MLA_NOPE + MLA_ROPE
    w_uq = jnp.pad(mla_w_uq[0].reshape(q_lora, MLA_HEADS, head_dim), ((0, 0), (0, 0), (0, HEAD_PAD - head_dim)))
    w_uq = w_uq.reshape(q_lora, MLA_HEADS * HEAD_PAD).astype(BF16)
    w_dkv_c = mla_w_dkv[:, :kv_lora].astype(BF16)
    w_dkv_p = jnp.pad(mla_w_dkv[:, kv_lora:], ((0, 0), (0, LANES - MLA_ROPE))).astype(BF16)
    pad_rope = lambda g: jnp.pad(g, (0, LANES - MLA_ROPE))[None]
    weights = [
        g_pre_mix[1][None], mla_g_kv_in[None], mla_w_dq[0].astype(BF16), mla_g_qa[0][None], w_uq,
        mla_g_qn[0][None], pad_rope(mla_g_qp[0]),
        w_dkv_c, w_dkv_p, mla_g_kv_a[None], pad_rope(mla_g_kpe),
        mla_w_uk.astype(BF16), mla_w_uv.astype(BF16), mla_g_kn[None],
    ]
    scale = float(head_dim ** -0.5 * 1.4426950408889634)
    qcat, kcat, vt, ckv, kpe = _mla_proj(h2, weights, cos_m, sin_m, n_prompt_tiles, tiles_per_seq, scale)
    o_attn_p = _prompt_attn(qcat, kcat, vt, batch, seq)
    qa, qps = _absorb_q(qcat, mla_g_kn[None], mla_w_uk.astype(BF16), n_prompt, n_sample)
    o_attn_s = _sample_attn(page_table, qa, qps, ckv, kpe, mla_w_uk.T.astype(BF16), mla_w_uv.astype(BF16),
                            cache_ckv, cache_kpe.transpose(0, 2, 1), n_prompt, dec_batch, dec_seq)

    w_r = jnp.pad(moe_w_router[0], ((0, 0), (0, LANES - N_EXPERTS))).astype(BF16)
    tri = (jnp.arange(tm)[:, None] >= jnp.arange(tm)[None, :]).astype(BF16)
    h3, xf, info, info_t, counts = _route(h2, o_attn_p, o_attn_s, mla_w_o[0].astype(BF16), g_pre_ffn[1][None],
                                          w_r, tri)

    te = EXPERT_TILE
    counts = counts[0, :N_EXPERTS].astype(jnp.int32)
    tiles_e = (counts + te - 1) // te
    tile_end = jnp.cumsum(tiles_e)
    offsets = (tile_end - tiles_e) * te
    n_tiles = (2 * n) // te + N_EXPERTS
    tile_ids = jnp.arange(n_tiles, dtype=jnp.int32)
    tile_valid = (tile_ids < tile_end[-1]).astype(jnp.int32)
    tile_expert = jnp.minimum(jnp.sum(tile_ids[:, None] >= tile_end[None, :], axis=1), N_EXPERTS - 1)
    last_expert = jnp.sum((tile_end[-1] - 1) >= tile_end).astype(jnp.int32)
    tile_expert = jnp.where(tile_valid == 1, tile_expert, last_expert).astype(jnp.int32)
    idx = info_t[0:2].astype(jnp.int32)
    base = jnp.zeros_like(idx)
    for e in range(N_EXPERTS):
        base = jnp.where(idx == e, offsets[e], base)
    dest = base + info_t[2:4].astype(jnp.int32)
    dest = dest.reshape(2, n // tm, tm).transpose(1, 0, 2).reshape(n // tm, 1, 2 * tm)

    xs = _scatter_rows(dest, xf, jnp.zeros((n_tiles * te, d), F32))
    ys = _experts(tile_expert, tile_valid, xs, moe_w_gu[0].astype(BF16), moe_w_down[0].astype(BF16))
    y_p, y_s = _combine(dest, h3, info, ys, n_prompt)

    return (
        y_p.reshape(batch, seq, d),
        y_s.reshape(dec_batch, dec_seq, d),
        s_prompt[None],
        s_sample[None],
        ckv[:n_prompt].reshape(batch, seq, kv_lora),
        kpe[:n_prompt].reshape(batch, seq, MLA_ROPE),
        ckv[n_prompt:].reshape(dec_batch, dec_seq, kv_lora),
        kpe[n_prompt:].reshape(dec_batch, dec_seq, MLA_ROPE),
    )
```

```python
import functools

import jax
import jax.numpy as jnp
from jax import lax
from jax.experimental import pallas as pl
from jax.experimental.pallas import tpu as pltpu

F32 = jnp.float32
BF16 = jnp.bfloat16
EPS = 1e-6
ROPE_BASE = 10000.0

RET_HEADS = 4
RET_CHUNK = 128
MLA_HEADS = 8
MLA_NOPE = 128
MLA_ROPE = 64
MLA_DV = 128
N_EXPERTS = 8
LANES = 128

TOKEN_TILE = 512
RET_ROWS = 512
ATTN_TILE = 512
ATTN_QBLOCK = 256
EXPERT_TILE = 1024
EXPERT_SUB = 256
FF_CHUNK = 512
PAGES_PER_CHUNK = 32
SAMPLE_KEY_BLOCK = 256
MIB = 1024 * 1024


def _params(semantics, vmem_mib, flags=None):
    return pltpu.CompilerParams(dimension_semantics=semantics, vmem_limit_bytes=vmem_mib * MIB, flags=flags)


def _resident(shape):
    return pl.BlockSpec(shape, lambda *_: (0,) * len(shape), pipeline_mode=pl.Buffered(1))


def _rms(x, g):
    ms = jnp.mean(x * x, axis=-1, keepdims=True)
    return (x * lax.rsqrt(ms + EPS)) * g


def _dot(a, b):
    return jnp.dot(a, b, preferred_element_type=F32)


def _dot_nt(a, b):
    return lax.dot_general(a, b, (((1,), (1,)), ((), ())), preferred_element_type=F32)


def _dot_tn(a, b):
    return lax.dot_general(a, b, (((0,), (0,)), ((), ())), preferred_element_type=F32)


def _silu(x):
    return x * jax.nn.sigmoid(x)


def _split_specs(tm, width, n_prompt_tiles):
    return [
        pl.BlockSpec((tm, width), lambda i: (jnp.minimum(i, n_prompt_tiles - 1), 0)),
        pl.BlockSpec((tm, width), lambda i: (jnp.maximum(i - n_prompt_tiles, 0), 0), pipeline_mode=pl.Buffered(1)),
    ]


def _pick(p_ref, s_ref, n_prompt_tiles):
    return jnp.where(pl.program_id(0) < n_prompt_tiles, p_ref[...], s_ref[...])


def _ret_in_kernel(xp_ref, xs_ref, g_ref, w_ref, cos_ref, sin_ref, q_ref, k_ref, v_ref, gate_ref,
                   *, hk, hv, dk, n_prompt_tiles):
    xn = _rms(_pick(xp_ref, xs_ref, n_prompt_tiles), g_ref[...]).astype(BF16)
    cos = cos_ref[...]
    sin = sin_ref[...]
    half = dk // 2

    def rope_store(dst, col, acc, scale):
        x1 = acc[:, :half]
        x2 = acc[:, half:]
        dst[:, col:col + half] = ((x1 * cos - x2 * sin) * scale).astype(BF16)
        dst[:, col + half:col + dk] = ((x1 * sin + x2 * cos) * scale).astype(BF16)

    for h in range(hk // dk):
        rope_store(q_ref, h * dk, _dot(xn, w_ref[:, h * dk:(h + 1) * dk]), 1.0)
    for h in range(hk // dk):
        rope_store(k_ref, h * dk, _dot(xn, w_ref[:, hk + h * dk:hk + (h + 1) * dk]), dk ** -0.5)
    step = 512
    for c in range(hv // step):
        v_ref[:, c * step:(c + 1) * step] = _dot(xn, w_ref[:, 2 * hk + c * step:2 * hk + (c + 1) * step]).astype(BF16)
    for c in range(hv // step):
        gate_ref[:, c * step:(c + 1) * step] = _dot(
            xn, w_ref[:, 2 * hk + hv + c * step:2 * hk + hv + (c + 1) * step]).astype(BF16)


def _ret_in(x_p, x_s, g, w, cos_tab, sin_tab, n_prompt_tiles, tiles_per_seq, hk, hv):
    d = x_p.shape[1]
    n = x_p.shape[0] + x_s.shape[0]
    tm = TOKEN_TILE
    dk = hk // RET_HEADS

    def tab_map(i):
        return (jnp.where(i < n_prompt_tiles, i % tiles_per_seq, tiles_per_seq), 0)

    return pl.pallas_call(
        functools.partial(_ret_in_kernel, hk=hk, hv=hv, dk=dk, n_prompt_tiles=n_prompt_tiles),
        grid=(n // tm,),
        in_specs=_split_specs(tm, d, n_prompt_tiles) + [
            _resident((1, d)),
            _resident(w.shape),
            pl.BlockSpec((tm, dk // 2), tab_map),
            pl.BlockSpec((tm, dk // 2), tab_map),
        ],
        out_specs=[
            pl.BlockSpec((tm, hk), lambda i: (i, 0)),
            pl.BlockSpec((tm, hk), lambda i: (i, 0)),
            pl.BlockSpec((tm, hv), lambda i: (i, 0)),
            pl.BlockSpec((tm, hv), lambda i: (i, 0)),
        ],
        out_shape=[
            jax.ShapeDtypeStruct((n, hk), BF16),
            jax.ShapeDtypeStruct((n, hk), BF16),
            jax.ShapeDtypeStruct((n, hv), BF16),
            jax.ShapeDtypeStruct((n, hv), BF16),
        ],
        compiler_params=_params(("parallel",), 48),
        name="ret_in",
    )(x_p, x_s, g, w, cos_tab, sin_tab)


def _group_norm(o):
    mu = jnp.mean(o, axis=-1, keepdims=True)
    var = jnp.mean(jnp.square(o - mu), axis=-1, keepdims=True)
    return (o - mu) * lax.rsqrt(var + EPS)


def _ret_prompt_kernel(q_ref, k_ref, v_ref, dmat_ref, dq_ref, dk_ref, gc_ref, o_ref, sfin_ref, state_ref,
                       *, dk, dv, chunk):
    j = pl.program_id(1)

    @pl.when(j == 0)
    def _():
        state_ref[...] = jnp.zeros_like(state_ref)

    def one_chunk(c, carry):
        r0 = pl.multiple_of(c * chunk, chunk)
        rows = pl.ds(r0, chunk)
        for h in range(RET_HEADS):
            q = q_ref[rows, h * dk:(h + 1) * dk]
            k = k_ref[rows, h * dk:(h + 1) * dk]
            v = v_ref[rows, h * dv:(h + 1) * dv]
            s = state_ref[h]
            inner = _dot_nt(q, k) * dmat_ref[h]
            o = _dot(inner.astype(BF16), v) + _dot(q, s.astype(BF16)) * dq_ref[h]
            kw = (k.astype(F32) * dk_ref[h]).astype(BF16)
            state_ref[h] = s * gc_ref[h] + _dot_tn(kw, v)
            o_ref[rows, h * dv:(h + 1) * dv] = _group_norm(o).astype(BF16)
        return carry

    lax.fori_loop(0, q_ref.shape[0] // chunk, one_chunk, 0)

    @pl.when(j == pl.num_programs(1) - 1)
    def _():
        sfin_ref[0] = state_ref[...]


def _ret_prompt(q, k, v, consts, batch, seq):
    hk = q.shape[1]
    hv = v.shape[1]
    dk, dv = hk // RET_HEADS, hv // RET_HEADS
    rb = RET_ROWS
    per_seq = seq // rb
    dmat, dq, dkk, gc = consts
    row = lambda b, j: (b * per_seq + j, 0)
    return pl.pallas_call(
        functools.partial(_ret_prompt_kernel, dk=dk, dv=dv, chunk=RET_CHUNK),
        grid=(batch, per_seq),
        in_specs=[
            pl.BlockSpec((rb, hk), row),
            pl.BlockSpec((rb, hk), row),
            pl.BlockSpec((rb, hv), row),
            _resident(dmat.shape),
            _resident(dq.shape),
            _resident(dkk.shape),
            _resident(gc.shape),
        ],
        out_specs=[
            pl.BlockSpec((rb, hv), row),
            pl.BlockSpec((1, RET_HEADS, dk, dv), lambda b, j: (b, 0, 0, 0)),
        ],
        out_shape=[
            jax.ShapeDtypeStruct((batch * seq, hv), BF16),
            jax.ShapeDtypeStruct((batch, RET_HEADS, dk, dv), F32),
        ],
        scratch_shapes=[pltpu.VMEM((RET_HEADS, dk, dv), F32)],
        compiler_params=_params(("parallel", "arbitrary"), 40),
        name="ret_prompt",
    )(q, k, v, dmat, dq, dkk, gc)


def _ret_sample_kernel(q_ref, k_ref, v_ref, s_ref, dmat_ref, dq_ref, dk_ref, gc_ref,
                       o_ref, snew_ref, *, dk, dv, dec_seq, seqs):
    rows = seqs * dec_seq
    row_seq = lax.broadcasted_iota(jnp.int32, (rows, 1), 0) // dec_seq
    for h in range(RET_HEADS):
        q = q_ref[:, h * dk:(h + 1) * dk]
        k = k_ref[:, h * dk:(h + 1) * dk]
        v = v_ref[:, h * dv:(h + 1) * dv]
        inner = _dot_nt(q, k) * dmat_ref[h]
        o = _dot(inner.astype(BF16), v)
        kw = k.astype(F32) * dk_ref[h]
        cross = jnp.zeros((rows, dv), F32)
        for s in range(seqs):
            st = s_ref[s, h]
            mine = row_seq == s
            cross = cross + jnp.where(mine, _dot(q, st.astype(BF16)), 0.0)
            kws = jnp.where(mine, kw, 0.0).astype(BF16)
            snew_ref[s, h] = st * gc_ref[h] + _dot_tn(kws, v)
        o = o + cross * dq_ref[h]
        o_ref[:, h * dv:(h + 1) * dv] = _group_norm(o).astype(BF16)


def _ret_sample(q, k, v, state, consts, n_prompt, dec_batch, dec_seq):
    hk = q.shape[1]
    hv = v.shape[1]
    dk, dv = hk // RET_HEADS, hv // RET_HEADS
    seqs = 16 // dec_seq if dec_seq < 16 else 1
    rows = seqs * dec_seq
    base = n_prompt // rows
    dmat, dq, dkk, gc = consts
    row = lambda i: (base + i, 0)
    return pl.pallas_call(
        functools.partial(_ret_sample_kernel, dk=dk, dv=dv, dec_seq=dec_seq, seqs=seqs),
        grid=(dec_batch // seqs,),
        in_specs=[
            pl.BlockSpec((rows, hk), row),
            pl.BlockSpec((rows, hk), row),
            pl.BlockSpec((rows, hv), row),
            pl.BlockSpec((seqs, RET_HEADS, dk, dv), lambda i: (i, 0, 0, 0)),
            _resident(dmat.shape),
            _resident(dq.shape),
            _resident(dkk.shape),
            _resident(gc.shape),
        ],
        out_specs=[
            pl.BlockSpec((rows, hv), lambda i: (i, 0)),
            pl.BlockSpec((seqs, RET_HEADS, dk, dv), lambda i: (i, 0, 0, 0)),
        ],
        out_shape=[
            jax.ShapeDtypeStruct((dec_batch * dec_seq, hv), BF16),
            jax.ShapeDtypeStruct(state.shape, F32),
        ],
        compiler_params=_params(("parallel",), 40),
        name="ret_sample",
    )(q, k, v, state, dmat, dq, dkk, gc)


def _decay_consts(chunk, rows):
    log_gamma = jnp.log1p(-jnp.exp2(-5.0 - jnp.arange(RET_HEADS, dtype=F32)))
    idx = jnp.arange(rows)
    pos = (idx % chunk).astype(F32)
    same = (idx[:, None] // chunk) == (idx[None, :] // chunk)
    diff = pos[:, None] - pos[None, :]
    lg = log_gamma[:, None, None]
    dmat = jnp.where(same & (diff >= 0), jnp.exp(lg * jnp.maximum(diff, 0.0)), 0.0)
    dq = jnp.exp(log_gamma[:, None] * (pos + 1.0))[..., None]
    dk = jnp.exp(log_gamma[:, None] * (chunk - 1.0 - pos))[..., None]
    gc = jnp.exp(log_gamma * chunk)[:, None, None]
    return dmat.astype(F32), dq.astype(F32), dk.astype(F32), gc.astype(F32)


def _mix_ffn_kernel(xp_ref, xs_ref, op_ref, os_ref, gate_ref, gout_ref, wout_ref, gffn_ref, wgu_ref, wdown_ref,
                    h_ref, act_ref, *, d_ff, chunk, n_prompt_tiles):
    o = _pick(op_ref, os_ref, n_prompt_tiles).astype(F32) * gout_ref[...]
    z = (_silu(gate_ref[...].astype(F32)) * o).astype(BF16)
    h1 = _pick(xp_ref, xs_ref, n_prompt_tiles) + _dot(z, wout_ref[...])
    xf = _rms(h1, gffn_ref[...]).astype(BF16)
    for c in range(d_ff // chunk):
        g = _dot(xf, wgu_ref[:, c * chunk:(c + 1) * chunk])
        u = _dot(xf, wgu_ref[:, d_ff + c * chunk:d_ff + (c + 1) * chunk])
        act_ref[:, c * chunk:(c + 1) * chunk] = (_silu(g) * u).astype(BF16)
    h_ref[...] = h1 + _dot(act_ref[...], wdown_ref[...])


def _mix_ffn(x_p, x_s, onorm_p, onorm_s, gate, g_out, w_out, g_ffn, w_gu, w_down):
    d = x_p.shape[1]
    n = x_p.shape[0] + x_s.shape[0]
    hv = onorm_p.shape[1]
    d_ff = w_down.shape[0]
    tm = TOKEN_TILE
    npt = onorm_p.shape[0] // tm
    return pl.pallas_call(
        functools.partial(_mix_ffn_kernel, d_ff=d_ff, chunk=256, n_prompt_tiles=npt),
        grid=(n // tm,),
        in_specs=_split_specs(tm, d, npt) + _split_specs(tm, hv, npt) + [
            pl.BlockSpec((tm, hv), lambda i: (i, 0)),
            _resident((1, hv)),
            _resident(w_out.shape),
            _resident((1, d)),
            _resident(w_gu.shape),
            _resident(w_down.shape),
        ],
        out_specs=pl.BlockSpec((tm, d), lambda i: (i, 0)),
        out_shape=jax.ShapeDtypeStruct((n, d), F32),
        scratch_shapes=[pltpu.VMEM((tm, d_ff), BF16)],
        compiler_params=_params(("parallel",), 56),
        name="mix_ffn",
    )(x_p, x_s, onorm_p, onorm_s, gate, g_out, w_out, g_ffn, w_gu, w_down)


def _rope_lanes(x, cos, sin_signed, half):
    width = x.shape[-1]
    lane = lax.broadcasted_iota(jnp.int32, x.shape, 1)
    first = (lane % (2 * half)) < half
    rot = jnp.where(first, pltpu.roll(x, width - half, axis=1), pltpu.roll(x, half, axis=1))
    return x * cos + rot * sin_signed


def _unit_rms(x, width):
    ms = jnp.sum(x * x, axis=-1, keepdims=True) * (1.0 / width)
    return x * lax.rsqrt(ms + EPS)


HEAD_PAD = 256
V_PAD = 144


def _mla_proj_kernel(h_ref, gmix_ref, gkv_ref, wdq_ref, gqa_ref, wuq_ref, gqn_ref, gqp_ref,
                     wdkvc_ref, wdkvp_ref, gkva_ref, gkpe_ref, wuk_ref, wuv_ref, gkn_ref,
                     cos_ref, sin_ref,
                     qcat_ref, kcat_ref, vt_ref, ckv_ref, kpe_ref, *, scale):
    h = h_ref[...]
    r = lax.rsqrt(jnp.mean(h * h, axis=-1, keepdims=True) + EPS)
    xn = ((h * r) * gmix_ref[...]).astype(BF16)
    xk = ((h * r) * gkv_ref[...]).astype(BF16)
    cos = cos_ref[...]
    sin = sin_ref[...]
    half = MLA_ROPE // 2

    ckv = _rms(_dot(xk, wdkvc_ref[...]), gkva_ref[...])
    ckv_ref[...] = ckv
    ap = _dot(xk, wdkvp_ref[...])
    kpe = _rope_lanes(_unit_rms(ap, MLA_ROPE) * gkpe_ref[...], cos, sin, half)
    kpe_ref[...] = kpe[:, :MLA_ROPE]
    kpe_b = kpe.astype(BF16)

    cq = _rms(_dot(xn, wdq_ref[...]), gqa_ref[...]).astype(BF16)
    q = _dot(cq, wuq_ref[...])
    cb = ckv.astype(BF16)
    kn = _dot(cb, wuk_ref[...])
    for hd in range(MLA_HEADS):
        c0 = hd * HEAD_PAD
        qn = _unit_rms(q[:, c0:c0 + MLA_NOPE], MLA_NOPE) * gqn_ref[...]
        qp = _unit_rms(q[:, c0 + MLA_NOPE:c0 + HEAD_PAD], MLA_ROPE) * gqp_ref[...]
        qp = _rope_lanes(qp, cos, sin, half)
        qcat_ref[:, c0:c0 + MLA_NOPE] = (qn * scale).astype(BF16)
        qcat_ref[:, c0 + MLA_NOPE:c0 + HEAD_PAD] = (qp * scale).astype(BF16)
        kh = _unit_rms(kn[:, hd * MLA_NOPE:(hd + 1) * MLA_NOPE], MLA_NOPE) * gkn_ref[...]
        kcat_ref[:, c0:c0 + MLA_NOPE] = kh.astype(BF16)
        kcat_ref[:, c0 + MLA_NOPE:c0 + HEAD_PAD] = kpe_b
    vt = _dot(cb, wuv_ref[...]).T
    ones = jnp.ones((V_PAD - MLA_DV, vt.shape[1]), BF16)
    for hd in range(MLA_HEADS):
        vt_ref[hd * V_PAD:hd * V_PAD + MLA_DV, :] = vt[hd * MLA_DV:(hd + 1) * MLA_DV, :].astype(BF16)
        vt_ref[hd * V_PAD + MLA_DV:(hd + 1) * V_PAD, :] = ones


def _mla_proj(h, weights, cos_tab, sin_tab, n_prompt_tiles, tiles_per_seq, scale):
    n, d = h.shape
    tm = TOKEN_TILE
    hc = MLA_HEADS * HEAD_PAD
    hn = MLA_HEADS * V_PAD
    kv_lora = weights[7].shape[1]

    def tab_map(i):
        return (jnp.where(i < n_prompt_tiles, i % tiles_per_seq, tiles_per_seq), 0)

    row = lambda i: (i, 0)
    in_specs = [pl.BlockSpec((tm, d), row)] + [_resident(w.shape) for w in weights]
    in_specs += [pl.BlockSpec((tm, LANES), tab_map), pl.BlockSpec((tm, LANES), tab_map)]
    return pl.pallas_call(
        functools.partial(_mla_proj_kernel, scale=scale),
        grid=(n // tm,),
        in_specs=in_specs,
        out_specs=[
            pl.BlockSpec((tm, hc), row),
            pl.BlockSpec((tm, hc), row),
            pl.BlockSpec((hn, tm), lambda i: (0, i)),
            pl.BlockSpec((tm, kv_lora), row),
            pl.BlockSpec((tm, MLA_ROPE), row),
        ],
        out_shape=[
            jax.ShapeDtypeStruct((n, hc), BF16),
            jax.ShapeDtypeStruct((n, hc), BF16),
            jax.ShapeDtypeStruct((hn, n), BF16),
            jax.ShapeDtypeStruct((n, kv_lora), F32),
            jax.ShapeDtypeStruct((n, MLA_ROPE), F32),
        ],
        compiler_params=_params(("parallel",), 48),
        name="mla_proj",
    )(h, *weights, cos_tab, sin_tab)


def _prompt_attn_kernel(q_ref, k_ref, vt_ref, o_ref, *scratch, tile, qblock):
    m_refs = scratch[:MLA_HEADS]
    acc_refs = scratch[MLA_HEADS:]
    qi = pl.program_id(1)
    ki = pl.program_id(2)

    @pl.when(ki == 0)
    def _():
        for h in range(MLA_HEADS):
            m_refs[h][...] = jnp.full_like(m_refs[h], -jnp.inf)
            acc_refs[h][...] = jnp.zeros_like(acc_refs[h])

    def key_tile(diagonal):
        if diagonal:
            key = lax.broadcasted_iota(jnp.int32, (qblock, tile), 0)
            qry = lax.broadcasted_iota(jnp.int32, (qblock, tile), 1)
        n_kb = tile // qblock
        for h in range(MLA_HEADS):
            cols = slice(h * HEAD_PAD, (h + 1) * HEAD_PAD)
            parts = []
            for j in range(n_kb):
                keys = slice(j * qblock, (j + 1) * qblock)
                s = _dot_nt(k_ref[keys, cols], q_ref[:, cols])
                if diagonal:
                    s = jnp.where(key + j * qblock <= qry, s, -jnp.inf)
                parts.append(s)
            m_old = m_refs[h][...]
            m_new = m_old
            for s in parts:
                m_new = jnp.maximum(m_new, jnp.max(s, axis=0, keepdims=True))
            m_refs[h][...] = m_new
            acc = jnp.exp2(m_old - m_new) * acc_refs[h][...]
            for j, s in enumerate(parts):
                p = jnp.exp2(s - m_new).astype(BF16)
                acc = acc + _dot(vt_ref[h * V_PAD:(h + 1) * V_PAD, j * qblock:(j + 1) * qblock], p)
            acc_refs[h][...] = acc

    @pl.when(ki < qi)
    def _():
        key_tile(False)

    @pl.when(ki == qi)
    def _():
        key_tile(True)
        outs = []
        for h in range(MLA_HEADS):
            blk = acc_refs[h][...]
            outs.append(blk[:MLA_DV] / blk[MLA_DV:MLA_DV + 1])
        o_ref[...] = jnp.concatenate(outs, axis=0).T


def _prompt_attn(qcat, kcat, vt, batch, seq):
    t = ATTN_TILE
    per_seq = seq // t
    hc = qcat.shape[1]
    hn = MLA_HEADS * MLA_DV
    hvp = vt.shape[0]
    qrow = lambda b, qi, ki: (b * per_seq + qi, 0)
    krow = lambda b, qi, ki: (b * per_seq + jnp.minimum(ki, qi), 0)
    vcol = lambda b, qi, ki: (0, b * per_seq + jnp.minimum(ki, qi))
    return pl.pallas_call(
        functools.partial(_prompt_attn_kernel, tile=t, qblock=ATTN_QBLOCK),
        grid=(batch, per_seq, per_seq),
        in_specs=[
            pl.BlockSpec((t, hc), qrow),
            pl.BlockSpec((t, hc), krow),
            pl.BlockSpec((hvp, t), vcol),
        ],
        out_specs=pl.BlockSpec((t, hn), qrow),
        out_shape=jax.ShapeDtypeStruct((batch * seq, hn), F32),
        scratch_shapes=([pltpu.VMEM((1, t), F32)] * MLA_HEADS + [pltpu.VMEM((V_PAD, t), F32)] * MLA_HEADS),
        compiler_params=_params(("parallel", "parallel", "arbitrary"), 40),
        name="prompt_attn",
    )(qcat, kcat, vt)


def _absorb_q_kernel(qcat_ref, gkn_ref, wuk_ref, qa_ref, qps_ref):
    lora = wuk_ref.shape[0]
    for h in range(MLA_HEADS):
        c0 = h * HEAD_PAD
        qg = (qcat_ref[:, c0:c0 + MLA_NOPE].astype(F32) * gkn_ref[...]).astype(BF16)
        qa_ref[:, h * lora:(h + 1) * lora] = _dot_nt(qg, wuk_ref[:, h * MLA_NOPE:(h + 1) * MLA_NOPE])
        qps_ref[:, h * LANES:(h + 1) * LANES] = qcat_ref[:, c0 + MLA_NOPE:c0 + HEAD_PAD].astype(F32)


def _absorb_q(qcat, gkn, wuk, n_prompt, n_sample):
    hc = qcat.shape[1]
    lora = wuk.shape[0]
    tm = min(256, n_sample)
    base = n_prompt // tm
    return pl.pallas_call(
        _absorb_q_kernel,
        grid=(n_sample // tm,),
        in_specs=[
            pl.BlockSpec((tm, hc), lambda i: (base + i, 0)),
            _resident(gkn.shape),
            _resident(wuk.shape),
        ],
        out_specs=[
            pl.BlockSpec((tm, MLA_HEADS * lora), lambda i: (i, 0)),
            pl.BlockSpec((tm, MLA_HEADS * LANES), lambda i: (i, 0)),
        ],
        out_shape=[
            jax.ShapeDtypeStruct((n_sample, MLA_HEADS * lora), F32),
            jax.ShapeDtypeStruct((n_sample, MLA_HEADS * LANES), F32),
        ],
        compiler_params=_params(("parallel",), 32),
        name="absorb_q",
    )(qcat, gkn, wuk)


def _sample_attn_kernel(pt_ref, qa_ref, qp_ref, cnew_ref, pnew_ref, wukt_ref, wuv_ref, cache_c_ref, cache_pt_ref,
                        o_ref, cbuf, pbuf, sem, m_ref, l_ref, acc_ref, lhs_s, qa_s, qp_s, cnew_s, pnew_s,
                        cbb0, cbb1, s_scr0, s_scr1, *, n_pages, ppc, page, dec_seq, lora, key_block):
    b = pl.program_id(0)
    nb = pl.num_programs(0)
    n_chunks = n_pages // ppc
    hq = MLA_HEADS * dec_seq
    n_proj = MLA_HEADS * MLA_NOPE

    def copies(seq, chunk, slot):
        out = []
        for p in range(ppc):
            pg = pt_ref[seq, chunk * ppc + p]
            out.append(pltpu.make_async_copy(cache_c_ref.at[pg], cbuf.at[slot, pl.ds(p * page, page)], sem.at[0, slot]))
            out.append(pltpu.make_async_copy(cache_pt_ref.at[pg], pbuf.at[slot, :, pl.ds(p * page, page)],
                                             sem.at[1, slot]))
        return out

    @pl.when(b == 0)
    def _():
        for cp in copies(0, 0, 0):
            cp.start()
        lhs_s[0:n_proj, :] = wukt_ref[...]

    for h in range(MLA_HEADS):
        qa_s[h * dec_seq:(h + 1) * dec_seq, :] = qa_ref[:, h * lora:(h + 1) * lora]
        qp_s[h * dec_seq:(h + 1) * dec_seq, :] = qp_ref[:, h * LANES:(h + 1) * LANES]
    lhs_s[n_proj:n_proj + hq, :] = qa_s[...].astype(BF16)
    qp = qp_s[...][:, :MLA_ROPE].astype(BF16)
    m_ref[...] = jnp.full_like(m_ref, -jnp.inf)
    l_ref[...] = jnp.zeros_like(l_ref)
    acc_ref[...] = jnp.zeros_like(acc_ref)

    def latent_scores(cb):
        t = cb.shape[0]
        kb = min(key_block, t)
        blocks = []
        for j in range(t // kb):
            kt = _dot_nt(lhs_s[...], cb[j * kb:(j + 1) * kb, :])
            parts = []
            for h in range(MLA_HEADS):
                kh = kt[h * MLA_NOPE:(h + 1) * MLA_NOPE, :]
                ms = jnp.sum(kh * kh, axis=0, keepdims=True) * (1.0 / MLA_NOPE)
                parts.append(kt[n_proj + h * dec_seq:n_proj + (h + 1) * dec_seq, :] * lax.rsqrt(ms + EPS))
            blocks.append(jnp.concatenate(parts, axis=0))
        return blocks[0] if len(blocks) == 1 else jnp.concatenate(blocks, axis=1)

    cbb = (cbb0, cbb1)
    s_scr = (s_scr0, s_scr1)

    def project(slot, par):
        cb = cbuf[slot].astype(BF16)
        cbb[par][...] = cb
        s_scr[par][...] = latent_scores(cb) + _dot(qp, pbuf[slot].astype(BF16))

    def absorb(s, cb):
        m_old = m_ref[...]
        m_new = jnp.maximum(m_old, jnp.max(s, axis=-1, keepdims=True))
        alpha = jnp.exp2(m_old - m_new)
        p = jnp.exp2(s - m_new)
        l_ref[...] = alpha * l_ref[...] + jnp.sum(p, axis=-1, keepdims=True)
        m_ref[...] = m_new
        acc_ref[...] = alpha * acc_ref[...] + _dot(p.astype(BF16), cb)

    def fetch_next(c, slot):
        if c + 1 < n_chunks:
            for cp in copies(b, c + 1, 1 - slot):
                cp.start()
        else:
            @pl.when(b + 1 < nb)
            def _():
                for cp in copies(b + 1, 0, 1 - slot):
                    cp.start()

        for cp in copies(b, c, slot):
            cp.wait()

    for c in range(n_chunks):
        slot = (b * n_chunks + c) % 2 if n_chunks % 2 else c % 2
        fetch_next(c, slot)
        project(slot, c % 2)
        if c:
            absorb(s_scr[(c - 1) % 2][...], cbb[(c - 1) % 2][...])

    cnew_s[...] = jnp.zeros_like(cnew_s)
    pnew_s[...] = jnp.zeros_like(pnew_s)
    cnew_s[0:dec_seq, :] = cnew_ref[...]
    pnew_s[0:dec_seq, :] = pnew_ref[...]
    cnew = cnew_s[...].astype(BF16)
    row_tok = lax.broadcasted_iota(jnp.int32, (hq, LANES), 0) % dec_seq
    col_tok = lax.broadcasted_iota(jnp.int32, (hq, LANES), 1)
    s_new = latent_scores(cnew) + _dot_nt(qp, pnew_s[...].astype(BF16))
    s_new = jnp.where(col_tok <= row_tok, s_new, -jnp.inf)
    last = (n_chunks - 1) % 2
    absorb(jnp.concatenate([s_scr[last][...], s_new], axis=1), jnp.concatenate([cbb[last][...], cnew], axis=0))

    lat = (acc_ref[...] / l_ref[...]).astype(BF16)
    full = _dot(lat, wuv_ref[...])
    for h in range(MLA_HEADS):
        cols = slice(h * MLA_DV, (h + 1) * MLA_DV)
        o_ref[:, cols] = full[h * dec_seq:(h + 1) * dec_seq, cols]


def _sample_attn(page_table, qa, qp, ckv, kpe, wukt, wuv, cache_c, cache_pt, n_prompt, dec_batch, dec_seq):
    n_pages = page_table.shape[1]
    page = cache_c.shape[1]
    lora = cache_c.shape[2]
    ppc = min(PAGES_PER_CHUNK, n_pages)
    hn = MLA_HEADS * MLA_DV
    hq = MLA_HEADS * dec_seq
    base = n_prompt // dec_seq
    srow = lambda b, pt: (b, 0)
    nrow = lambda b, pt: (base + b, 0)
    grid_spec = pltpu.PrefetchScalarGridSpec(
        num_scalar_prefetch=1,
        grid=(dec_batch,),
        in_specs=[
            pl.BlockSpec((dec_seq, MLA_HEADS * lora), srow),
            pl.BlockSpec((dec_seq, MLA_HEADS * LANES), srow),
            pl.BlockSpec((dec_seq, lora), nrow),
            pl.BlockSpec((dec_seq, MLA_ROPE), nrow),
            pl.BlockSpec(wukt.shape, lambda b, pt: (0, 0), pipeline_mode=pl.Buffered(1)),
            pl.BlockSpec(wuv.shape, lambda b, pt: (0, 0), pipeline_mode=pl.Buffered(1)),
            pl.BlockSpec(memory_space=pl.ANY),
            pl.BlockSpec(memory_space=pl.ANY),
        ],
        out_specs=pl.BlockSpec((dec_seq, hn), srow),
        scratch_shapes=[
            pltpu.VMEM((2, ppc * page, lora), F32),
            pltpu.VMEM((2, MLA_ROPE, ppc * page), F32),
            pltpu.SemaphoreType.DMA((2, 2)),
            pltpu.VMEM((hq, 1), F32),
            pltpu.VMEM((hq, 1), F32),
            pltpu.VMEM((hq, lora), F32),
            pltpu.VMEM((MLA_HEADS * MLA_NOPE + hq, lora), BF16),
            pltpu.VMEM((hq, lora), F32),
            pltpu.VMEM((hq, LANES), F32),
            pltpu.VMEM((LANES, lora), F32),
            pltpu.VMEM((LANES, MLA_ROPE), F32),
            pltpu.VMEM((ppc * page, lora), BF16),
            pltpu.VMEM((ppc * page, lora), BF16),
            pltpu.VMEM((hq, ppc * page), F32),
            pltpu.VMEM((hq, ppc * page), F32),
        ],
    )
    return pl.pallas_call(
        functools.partial(_sample_attn_kernel, n_pages=n_pages, ppc=ppc, page=page, dec_seq=dec_seq, lora=lora,
                          key_block=SAMPLE_KEY_BLOCK),
        grid_spec=grid_spec,
        out_shape=jax.ShapeDtypeStruct((dec_batch * dec_seq, hn), F32),
        compiler_params=_params(("arbitrary",), 40),
        name="sample_attn",
    )(page_table, qa, qp, ckv, kpe, wukt, wuv, cache_c, cache_pt)


def _route_kernel(h_ref, op_ref, os_ref, wo_ref, gffn_ref, wr_ref, tri_ref, h3_ref, xf_ref, info_ref, infot_ref,
                  cnt_ref, carry_ref, *, n_prompt_tiles):
    i = pl.program_id(0)

    @pl.when(i == 0)
    def _():
        carry_ref[...] = jnp.zeros_like(carry_ref)

    o = jnp.where(i < n_prompt_tiles, op_ref[...], os_ref[...])
    h3 = h_ref[...] + _dot(o.astype(BF16), wo_ref[...])
    h3_ref[...] = h3
    xf = _rms(h3, gffn_ref[...])
    xf_ref[...] = xf
    logits = _dot(xf.astype(BF16), wr_ref[...])
    lane = lax.broadcasted_iota(jnp.int32, logits.shape, 1).astype(F32)
    lg = jnp.where(lane < N_EXPERTS, logits, -jnp.inf)
    v1 = jnp.max(lg, axis=-1, keepdims=True)
    i1 = jnp.min(jnp.where(lg == v1, lane, float(LANES)), axis=-1, keepdims=True)
    lg2 = jnp.where(lane == i1, -jnp.inf, lg)
    v2 = jnp.max(lg2, axis=-1, keepdims=True)
    i2 = jnp.min(jnp.where(lg2 == v2, lane, float(LANES)), axis=-1, keepdims=True)
    e = jnp.exp(v2 - v1)
    g1 = 1.0 / (1.0 + e)
    g2 = e / (1.0 + e)
    oh1 = lane == i1
    oh2 = lane == i2
    chosen = jnp.where(oh1 | oh2, 1.0, 0.0)
    cum = _dot(tri_ref[...], chosen.astype(BF16))
    before = cum - chosen + carry_ref[...]
    pos1 = jnp.sum(jnp.where(oh1, before, 0.0), axis=-1, keepdims=True)
    pos2 = jnp.sum(jnp.where(oh2, before, 0.0), axis=-1, keepdims=True)
    carry_ref[...] = carry_ref[...] + cum[cum.shape[0] - 1:, :]
    cnt_ref[...] = carry_ref[...]
    info = jnp.where(lane == 0, i1, 0.0)
    info = jnp.where(lane == 1, i2, info)
    info = jnp.where(lane == 2, pos1, info)
    info = jnp.where(lane == 3, pos2, info)
    info = jnp.where(lane == 4, g1, info)
    info = jnp.where(lane == 5, g2, info)
    info_ref[...] = info
    infot_ref[...] = info.T[:8, :]


def _route(h2, o_attn_p, o_attn_s, w_o, g_ffn, w_r, tri):
    n, d = h2.shape
    tm = TOKEN_TILE
    npt = o_attn_p.shape[0] // tm
    hn = o_attn_p.shape[1]
    row = lambda i: (i, 0)
    return pl.pallas_call(
        functools.partial(_route_kernel, n_prompt_tiles=npt),
        grid=(n // tm,),
        in_specs=[
            pl.BlockSpec((tm, d), row),
            pl.BlockSpec((tm, hn), lambda i: (jnp.minimum(i, npt - 1), 0)),
            pl.BlockSpec((tm, hn), lambda i: (jnp.maximum(i - npt, 0), 0)),
            _resident(w_o.shape),
            _resident((1, d)),
            _resident(w_r.shape),
            _resident(tri.shape),
        ],
        out_specs=[
            pl.BlockSpec((tm, d), row),
            pl.BlockSpec((tm, d), row),
            pl.BlockSpec((tm, LANES), row),
            pl.BlockSpec((8, tm), lambda i: (0, i)),
            pl.BlockSpec((1, LANES), lambda i: (0, 0)),
        ],
        out_shape=[
            jax.ShapeDtypeStruct((n, d), F32),
            jax.ShapeDtypeStruct((n, d), F32),
            jax.ShapeDtypeStruct((n, LANES), F32),
            jax.ShapeDtypeStruct((8, n), F32),
            jax.ShapeDtypeStruct((1, LANES), F32),
        ],
        scratch_shapes=[pltpu.VMEM((1, LANES), F32)],
        compiler_params=_params(("arbitrary",), 40),
        name="route",
    )(h2, o_attn_p, o_attn_s, w_o, g_ffn, w_r, tri)


def _scatter_kernel(dest_ref, x_ref, xs_in_ref, xs_ref, sem, *, tm):
    del xs_in_ref

    def issue(r, carry):
        for k in range(2):
            d = dest_ref[0, 0, k * tm + r]
            pltpu.make_async_copy(x_ref.at[pl.ds(r, 1)], xs_ref.at[pl.ds(d, 1)], sem).start(priority=k)
        return carry

    lax.fori_loop(0, tm, issue, 0, unroll=8)

    def drain(r, carry):
        for k in range(2):
            pltpu.make_async_copy(x_ref.at[pl.ds(r, 1)], xs_ref.at[pl.ds(0, 1)], sem).wait()
        return carry

    lax.fori_loop(0, tm, drain, 0, unroll=8)


def _scatter_rows(dest, xf, xs0):
    n, d = xf.shape
    tm = TOKEN_TILE
    return pl.pallas_call(
        functools.partial(_scatter_kernel, tm=tm),
        grid=(n // tm,),
        in_specs=[
            pl.BlockSpec((1, 1, 2 * tm), lambda i: (i, 0, 0), memory_space=pltpu.SMEM),
            pl.BlockSpec((tm, d), lambda i: (i, 0)),
            pl.BlockSpec(memory_space=pl.ANY),
        ],
        out_specs=pl.BlockSpec(memory_space=pl.ANY),
        out_shape=jax.ShapeDtypeStruct(xs0.shape, xs0.dtype),
        scratch_shapes=[pltpu.SemaphoreType.DMA(())],
        input_output_aliases={2: 0},
        compiler_params=_params(("arbitrary",), 32),
        name="scatter_rows",
    )(dest, xf, xs0)


def _experts_kernel(te_ref, tr_ref, x_ref, wg_ref, wu_ref, wd_ref, o_ref, acc_ref, xb_ref, *, sub):
    t = pl.program_id(0)
    c = pl.program_id(1)
    rows = tr_ref[t]
    n_sub = x_ref.shape[0] // sub
    last = pl.num_programs(1) - 1

    @pl.when((c == 0) & (rows > 0))
    def _():
        xb_ref[...] = x_ref[...].astype(BF16)

    wg = wg_ref[0].astype(BF16)
    wu = wu_ref[0].astype(BF16)
    wd = wd_ref[0].astype(BF16)
    for sb in range(n_sub):
        r = slice(sb * sub, (sb + 1) * sub)

        @pl.when(sb * sub < rows)
        def _():
            xb = xb_ref[r, :]
            act = (_silu(_dot(xb, wg)) * _dot(xb, wu)).astype(BF16)
            part = _dot(act, wd)

            @pl.when(c == 0)
            def _():
                acc_ref[r, :] = part

            @pl.when(c > 0)
            def _():
                acc_ref[r, :] += part

            @pl.when(c == last)
            def _():
                o_ref[r, :] = acc_ref[r, :]

        @pl.when((sb * sub >= rows) & (c == last))
        def _():
            o_ref[r, :] = jnp.zeros((sub, o_ref.shape[1]), o_ref.dtype)


def _experts(tile_expert, tile_rows, xs, w_gu, w_down):
    rows, d = xs.shape
    tm = EXPERT_TILE
    fc = FF_CHUNK
    d_ff = w_down.shape[1]
    n_c = d_ff // fc
    n_tiles = rows // tm

    def ceff(t, c, tr):
        return jnp.where(tr[t] > 0, c, n_c - 1)

    grid_spec = pltpu.PrefetchScalarGridSpec(
        num_scalar_prefetch=2,
        grid=(n_tiles, n_c),
        in_specs=[
            pl.BlockSpec((tm, d), lambda t, c, te, tv: (t, 0)),
            pl.BlockSpec((1, d, fc), lambda t, c, te, tv: (te[t], 0, ceff(t, c, tv))),
            pl.BlockSpec((1, d, fc), lambda t, c, te, tv: (te[t], 0, n_c + ceff(t, c, tv))),
            pl.BlockSpec((1, fc, d), lambda t, c, te, tv: (te[t], ceff(t, c, tv), 0)),
        ],
        out_specs=pl.BlockSpec((tm, d), lambda t, c, te, tv: (t, 0)),
        scratch_shapes=[pltpu.VMEM((tm, d), F32), pltpu.VMEM((tm, d), BF16)],
    )
    return pl.pallas_call(
        functools.partial(_experts_kernel, sub=EXPERT_SUB),
        grid_spec=grid_spec,
        out_shape=jax.ShapeDtypeStruct((rows, d), F32),
        compiler_params=_params(("arbitrary", "arbitrary"), 52),
        name="experts",
    )(tile_expert, tile_rows, xs, w_gu, w_gu, w_down)


def _combine_kernel(dest_ref, h_ref, info_ref, ys_ref, yp_ref, ysm_ref, buf, sem, *, tm, n_prompt_tiles):
    def issue(r, carry):
        for k in range(2):
            d = dest_ref[0, 0, k * tm + r]
            pltpu.make_async_copy(ys_ref.at[pl.ds(d, 1)], buf.at[k, pl.ds(r, 1)], sem).start(priority=k)
        return carry

    lax.fori_loop(0, tm, issue, 0, unroll=8)

    def drain(r, carry):
        for k in range(2):
            pltpu.make_async_copy(ys_ref.at[pl.ds(0, 1)], buf.at[k, pl.ds(r, 1)], sem).wait()
        return carry

    lax.fori_loop(0, tm, drain, 0, unroll=8)
    info = info_ref[...]
    y = h_ref[...] + info[:, 4:5] * buf[0] + info[:, 5:6] * buf[1]

    @pl.when(pl.program_id(0) < n_prompt_tiles)
    def _():
        yp_ref[...] = y

    @pl.when(pl.program_id(0) >= n_prompt_tiles)
    def _():
        ysm_ref[...] = y


def _combine(dest, h3, info, ys, n_prompt):
    n, d = h3.shape
    tm = TOKEN_TILE
    npt = n_prompt // tm
    return pl.pallas_call(
        functools.partial(_combine_kernel, tm=tm, n_prompt_tiles=npt),
        grid=(n // tm,),
        in_specs=[
            pl.BlockSpec((1, 1, 2 * tm), lambda i: (i, 0, 0), memory_space=pltpu.SMEM),
            pl.BlockSpec((tm, d), lambda i: (i, 0)),
            pl.BlockSpec((tm, LANES), lambda i: (i, 0)),
            pl.BlockSpec(memory_space=pl.ANY),
        ],
        out_specs=[
            pl.BlockSpec((tm, d), lambda i: (jnp.minimum(i, npt - 1), 0)),
            pl.BlockSpec((tm, d), lambda i: (jnp.maximum(i - npt, 0), 0)),
        ],
        out_shape=[
            jax.ShapeDtypeStruct((n_prompt, d), F32),
            jax.ShapeDtypeStruct((n - n_prompt, d), F32),
        ],
        scratch_shapes=[pltpu.VMEM((2, tm, d), F32), pltpu.SemaphoreType.DMA(())],
        compiler_params=_params(("arbitrary",), 32),
        name="combine",
    )(dest, h3, info, ys)


def _rope_tables(pos, dim):
    inv = ROPE_BASE ** (-jnp.arange(0, dim, 2, dtype=F32) / dim)
    ang = pos.astype(F32)[:, None] * inv[None, :]
    return jnp.cos(ang), jnp.sin(ang)


def kernel(x_prompt, x_sample, state_ret, cache_ckv, cache_kpe, page_table, g_pre_mix, g_pre_ffn, ret_w_in, ret_g_out, ret_w_out, mla_g_kv_in, mla_w_dkv, mla_g_kv_a, mla_g_kpe, mla_w_uk, mla_w_uv, mla_g_kn, mla_w_dq, mla_g_qa, mla_w_uq, mla_g_qn, mla_g_qp, mla_w_o, ffn_w_gu, ffn_w_down, moe_w_router, moe_w_gu, moe_w_down):
    batch, seq, d = x_prompt.shape
    dec_batch, dec_seq, _ = x_sample.shape
    n_prompt = batch * seq
    n_sample = dec_batch * dec_seq
    n = n_prompt + n_sample
    tm = TOKEN_TILE
    past = page_table.shape[1] * cache_ckv.shape[1]
    hv = ret_g_out.shape[1]
    hk = (ret_w_in.shape[2] - 2 * hv) // 2
    kv_lora = mla_g_kv_a.shape[0]
    assert seq % tm == 0 and n_sample % tm == 0 and seq % ATTN_TILE == 0 and seq % RET_ROWS == 0
    n_prompt_tiles = n_prompt // tm
    tiles_per_seq = seq // tm

    x_p = x_prompt.reshape(n_prompt, d)
    x_s = x_sample.reshape(n_sample, d)

    pos = jnp.concatenate([jnp.arange(seq), past + (jnp.arange(tm) % dec_seq)])
    cos_r, sin_r = _rope_tables(pos, hk // RET_HEADS)
    cos_m, sin_m = _rope_tables(pos, MLA_ROPE)
    cos_m = jnp.tile(cos_m, (1, 2 * LANES // MLA_ROPE))
    sin_m = jnp.tile(jnp.concatenate([-sin_m, sin_m], axis=-1), (1, LANES // MLA_ROPE))

    q, k, v, gate = _ret_in(x_p, x_s, g_pre_mix[0][None], ret_w_in[0].astype(BF16), cos_r, sin_r,
                            n_prompt_tiles, tiles_per_seq, hk, hv)
    onorm_p, s_prompt = _ret_prompt(q, k, v, _decay_consts(RET_CHUNK, RET_CHUNK), batch, seq)
    seqs = 16 // dec_seq if dec_seq < 16 else 1
    onorm_s, s_sample = _ret_sample(q, k, v, state_ret[0], _decay_consts(dec_seq, seqs * dec_seq),
                                    n_prompt, dec_batch, dec_seq)
    h2 = _mix_ffn(x_p, x_s, onorm_p, onorm_s, gate, ret_g_out[0][None], ret_w_out[0].astype(BF16),
                  g_pre_ffn[0][None], ffn_w_gu[0].astype(BF16), ffn_w_down[0].astype(BF16))

    q_lora = mla_w_dq.shape[2]
    head_dim = MLA_NOPE + MLA_ROPE
    w_uq = jnp.pad(mla_w_uq[0].reshape(q_lora, MLA_HEADS, head_dim), ((0, 0), (0, 0), (0, HEAD_PAD - head_dim)))
    w_uq = w_uq.reshape(q_lora, MLA_HEADS * HEAD_PAD).astype(BF16)
    w_dkv_c = mla_w_dkv[:, :kv_lora].astype(BF16)
    w_dkv_p = jnp.pad(mla_w_dkv[:, kv_lora:], ((0, 0), (0, LANES - MLA_ROPE))).astype(BF16)
    pad_rope = lambda g: jnp.pad(g, (0, LANES - MLA_ROPE))[None]
    weights = [
        g_pre_mix[1][None], mla_g_kv_in[None], mla_w_dq[0].astype(BF16), mla_g_qa[0][None], w_uq,
        mla_g_qn[0][None], pad_rope(mla_g_qp[0]),
        w_dkv_c, w_dkv_p, mla_g_kv_a[None], pad_rope(mla_g_kpe),
        mla_w_uk.astype(BF16), mla_w_uv.astype(BF16), mla_g_kn[None],
    ]
    scale = float(head_dim ** -0.5 * 1.4426950408889634)
    qcat, kcat, vt, ckv, kpe = _mla_proj(h2, weights, cos_m, sin_m, n_prompt_tiles, tiles_per_seq, scale)
    o_attn_p = _prompt_attn(qcat, kcat, vt, batch, seq)
    qa, qps = _absorb_q(qcat, mla_g_kn[None], mla_w_uk.astype(BF16), n_prompt, n_sample)
    o_attn_s = _sample_attn(page_table, qa, qps, ckv, kpe, mla_w_uk.T.astype(BF16), mla_w_uv.astype(BF16),
                            cache_ckv, cache_kpe.transpose(0, 2, 1), n_prompt, dec_batch, dec_seq)

    w_r = jnp.pad(moe_w_router[0], ((0, 0), (0, LANES - N_EXPERTS))).astype(BF16)
    tri = (jnp.arange(tm)[:, None] >= jnp.arange(tm)[None, :]).astype(BF16)
    h3, xf, info, info_t, counts = _route(h2, o_attn_p, o_attn_s, mla_w_o[0].astype(BF16), g_pre_ffn[1][None],
                                          w_r, tri)

    te = EXPERT_TILE
    counts = counts[0, :N_EXPERTS].astype(jnp.int32)
    tiles_e = (counts + te - 1) // te
    tile_end = jnp.cumsum(tiles_e)
    offsets = (tile_end - tiles_e) * te
    n_tiles = (2 * n) // te + N_EXPERTS
    tile_ids = jnp.arange(n_tiles, dtype=jnp.int32)
    tile_valid = tile_ids < tile_end[-1]
    tile_expert = jnp.minimum(jnp.sum(tile_ids[:, None] >= tile_end[None, :], axis=1), N_EXPERTS - 1)
    first_tile = jnp.zeros_like(tile_ids)
    count_of = jnp.zeros_like(tile_ids)
    for e in range(N_EXPERTS):
        first_tile = jnp.where(tile_expert == e, tile_end[e] - tiles_e[e], first_tile)
        count_of = jnp.where(tile_expert == e, counts[e], count_of)
    tile_rows = jnp.where(tile_valid, jnp.clip(count_of - (tile_ids - first_tile) * te, 0, te), 0).astype(jnp.int32)
    last_expert = jnp.sum((tile_end[-1] - 1) >= tile_end).astype(jnp.int32)
    tile_expert = jnp.where(tile_valid, tile_expert, last_expert).astype(jnp.int32)
    idx = info_t[0:2].astype(jnp.int32)
    base = jnp.zeros_like(idx)
    for e in range(N_EXPERTS):
        base = jnp.where(idx == e, offsets[e], base)
    dest = base + info_t[2:4].astype(jnp.int32)
    dest = dest.reshape(2, n // tm, tm).transpose(1, 0, 2).reshape(n // tm, 1, 2 * tm)

    xs = _scatter_rows(dest, xf, jnp.zeros((n_tiles * te, d), F32))
    ys = _experts(tile_expert, tile_rows, xs, moe_w_gu[0], moe_w_down[0])
    y_p, y_s = _combine(dest, h3, info, ys, n_prompt)

    return (
        y_p.reshape(batch, seq, d),
        y_s.reshape(dec_batch, dec_seq, d),
        s_prompt[None],
        s_sample[None],
        ckv[:n_prompt].reshape(batch, seq, kv_lora),
        kpe[:n_prompt].reshape(batch, seq, MLA_ROPE),
        ckv[n_prompt:].reshape(dec_batch, dec_seq, kv_lora),
        kpe[n_prompt:].reshape(dec_batch, dec_seq, MLA_ROPE),
    )
```

```python
import functools

import jax
import jax.numpy as jnp
from jax import lax
from jax.experimental import pallas as pl
from jax.experimental.pallas import tpu as pltpu

F32 = jnp.float32
BF16 = jnp.bfloat16
EPS = 1e-6
ROPE_BASE = 10000.0

RET_HEADS = 4
RET_CHUNK = 128
MLA_HEADS = 8
MLA_NOPE = 128
MLA_ROPE = 64
MLA_DV = 128
N_EXPERTS = 8
LANES = 128

TOKEN_TILE = 512
RET_ROWS = 512
ATTN_TILE = 512
ATTN_QBLOCK = 256
EXPERT_TILE = 1024
EXPERT_SUB = 256
FF_CHUNK = 512
PAGES_PER_CHUNK = 32
SAMPLE_KEY_BLOCK = 256
MIB = 1024 * 1024


def _params(semantics, vmem_mib, flags=None):
    return pltpu.CompilerParams(dimension_semantics=semantics, vmem_limit_bytes=vmem_mib * MIB, flags=flags)


def _resident(shape):
    return pl.BlockSpec(shape, lambda *_: (0,) * len(shape), pipeline_mode=pl.Buffered(1))


def _rms(x, g):
    ms = jnp.mean(x * x, axis=-1, keepdims=True)
    return (x * lax.rsqrt(ms + EPS)) * g


def _dot(a, b):
    return jnp.dot(a, b, preferred_element_type=F32)


def _dot_nt(a, b):
    return lax.dot_general(a, b, (((1,), (1,)), ((), ())), preferred_element_type=F32)


def _dot_tn(a, b):
    return lax.dot_general(a, b, (((0,), (0,)), ((), ())), preferred_element_type=F32)


def _silu(x):
    return x * jax.nn.sigmoid(x)


def _split_specs(tm, width, n_prompt_tiles):
    return [
        pl.BlockSpec((tm, width), lambda i: (jnp.minimum(i, n_prompt_tiles - 1), 0)),
        pl.BlockSpec((tm, width), lambda i: (jnp.maximum(i - n_prompt_tiles, 0), 0), pipeline_mode=pl.Buffered(1)),
    ]


def _pick(p_ref, s_ref, n_prompt_tiles):
    return jnp.where(pl.program_id(0) < n_prompt_tiles, p_ref[...], s_ref[...])


def _ret_in_kernel(xp_ref, xs_ref, g_ref, w_ref, cos_ref, sin_ref, q_ref, k_ref, v_ref, gate_ref,
                   *, hk, hv, dk, n_prompt_tiles):
    xn = _rms(_pick(xp_ref, xs_ref, n_prompt_tiles), g_ref[...]).astype(BF16)
    cos = cos_ref[...]
    sin = sin_ref[...]
    half = dk // 2

    def rope_store(dst, col, acc, scale):
        x1 = acc[:, :half]
        x2 = acc[:, half:]
        dst[:, col:col + half] = ((x1 * cos - x2 * sin) * scale).astype(BF16)
        dst[:, col + half:col + dk] = ((x1 * sin + x2 * cos) * scale).astype(BF16)

    for h in range(hk // dk):
        rope_store(q_ref, h * dk, _dot(xn, w_ref[:, h * dk:(h + 1) * dk]), 1.0)
    for h in range(hk // dk):
        rope_store(k_ref, h * dk, _dot(xn, w_ref[:, hk + h * dk:hk + (h + 1) * dk]), dk ** -0.5)
    step = 512
    for c in range(hv // step):
        v_ref[:, c * step:(c + 1) * step] = _dot(xn, w_ref[:, 2 * hk + c * step:2 * hk + (c + 1) * step]).astype(BF16)
    for c in range(hv // step):
        gate_ref[:, c * step:(c + 1) * step] = _dot(
            xn, w_ref[:, 2 * hk + hv + c * step:2 * hk + hv + (c + 1) * step]).astype(BF16)


def _ret_in(x_p, x_s, g, w, cos_tab, sin_tab, n_prompt_tiles, tiles_per_seq, hk, hv):
    d = x_p.shape[1]
    n = x_p.shape[0] + x_s.shape[0]
    tm = TOKEN_TILE
    dk = hk // RET_HEADS

    def tab_map(i):
        return (jnp.where(i < n_prompt_tiles, i % tiles_per_seq, tiles_per_seq), 0)

    return pl.pallas_call(
        functools.partial(_ret_in_kernel, hk=hk, hv=hv, dk=dk, n_prompt_tiles=n_prompt_tiles),
        grid=(n // tm,),
        in_specs=_split_specs(tm, d, n_prompt_tiles) + [
            _resident((1, d)),
            _resident(w.shape),
            pl.BlockSpec((tm, dk // 2), tab_map),
            pl.BlockSpec((tm, dk // 2), tab_map),
        ],
        out_specs=[
            pl.BlockSpec((tm, hk), lambda i: (i, 0)),
            pl.BlockSpec((tm, hk), lambda i: (i, 0)),
            pl.BlockSpec((tm, hv), lambda i: (i, 0)),
            pl.BlockSpec((tm, hv), lambda i: (i, 0)),
        ],
        out_shape=[
            jax.ShapeDtypeStruct((n, hk), BF16),
            jax.ShapeDtypeStruct((n, hk), BF16),
            jax.ShapeDtypeStruct((n, hv), BF16),
            jax.ShapeDtypeStruct((n, hv), BF16),
        ],
        compiler_params=_params(("parallel",), 48),
        name="ret_in",
    )(x_p, x_s, g, w, cos_tab, sin_tab)


def _group_norm(o):
    mu = jnp.mean(o, axis=-1, keepdims=True)
    var = jnp.mean(jnp.square(o - mu), axis=-1, keepdims=True)
    return (o - mu) * lax.rsqrt(var + EPS)


def _ret_prompt_kernel(q_ref, k_ref, v_ref, dmat_ref, dq_ref, dk_ref, gc_ref, o_ref, sfin_ref, state_ref,
                       *, dk, dv, chunk):
    j = pl.program_id(1)

    @pl.when(j == 0)
    def _():
        state_ref[...] = jnp.zeros_like(state_ref)

    def one_chunk(c, carry):
        r0 = pl.multiple_of(c * chunk, chunk)
        rows = pl.ds(r0, chunk)
        for h in range(RET_HEADS):
            q = q_ref[rows, h * dk:(h + 1) * dk]
            k = k_ref[rows, h * dk:(h + 1) * dk]
            v = v_ref[rows, h * dv:(h + 1) * dv]
            s = state_ref[h]
            inner = _dot_nt(q, k) * dmat_ref[h]
            o = _dot(inner.astype(BF16), v) + _dot(q, s.astype(BF16)) * dq_ref[h]
            kw = (k.astype(F32) * dk_ref[h]).astype(BF16)
            state_ref[h] = s * gc_ref[h] + _dot_tn(kw, v)
            o_ref[rows, h * dv:(h + 1) * dv] = _group_norm(o).astype(BF16)
        return carry

    lax.fori_loop(0, q_ref.shape[0] // chunk, one_chunk, 0)

    @pl.when(j == pl.num_programs(1) - 1)
    def _():
        sfin_ref[0] = state_ref[...]


def _ret_prompt(q, k, v, consts, batch, seq):
    hk = q.shape[1]
    hv = v.shape[1]
    dk, dv = hk // RET_HEADS, hv // RET_HEADS
    rb = RET_ROWS
    per_seq = seq // rb
    dmat, dq, dkk, gc = consts
    row = lambda b, j: (b * per_seq + j, 0)
    return pl.pallas_call(
        functools.partial(_ret_prompt_kernel, dk=dk, dv=dv, chunk=RET_CHUNK),
        grid=(batch, per_seq),
        in_specs=[
            pl.BlockSpec((rb, hk), row),
            pl.BlockSpec((rb, hk), row),
            pl.BlockSpec((rb, hv), row),
            _resident(dmat.shape),
            _resident(dq.shape),
            _resident(dkk.shape),
            _resident(gc.shape),
        ],
        out_specs=[
            pl.BlockSpec((rb, hv), row),
            pl.BlockSpec((1, RET_HEADS, dk, dv), lambda b, j: (b, 0, 0, 0)),
        ],
        out_shape=[
            jax.ShapeDtypeStruct((batch * seq, hv), BF16),
            jax.ShapeDtypeStruct((batch, RET_HEADS, dk, dv), F32),
        ],
        scratch_shapes=[pltpu.VMEM((RET_HEADS, dk, dv), F32)],
        compiler_params=_params(("parallel", "arbitrary"), 40),
        name="ret_prompt",
    )(q, k, v, dmat, dq, dkk, gc)


def _ret_sample_kernel(q_ref, k_ref, v_ref, s_ref, dmat_ref, dq_ref, dk_ref, gc_ref,
                       o_ref, snew_ref, *, dk, dv, dec_seq, seqs):
    rows = seqs * dec_seq
    row_seq = lax.broadcasted_iota(jnp.int32, (rows, 1), 0) // dec_seq
    for h in range(RET_HEADS):
        q = q_ref[:, h * dk:(h + 1) * dk]
        k = k_ref[:, h * dk:(h + 1) * dk]
        v = v_ref[:, h * dv:(h + 1) * dv]
        inner = _dot_nt(q, k) * dmat_ref[h]
        o = _dot(inner.astype(BF16), v)
        kw = k.astype(F32) * dk_ref[h]
        cross = jnp.zeros((rows, dv), F32)
        for s in range(seqs):
            st = s_ref[s, h]
            mine = row_seq == s
            cross = cross + jnp.where(mine, _dot(q, st.astype(BF16)), 0.0)
            kws = jnp.where(mine, kw, 0.0).astype(BF16)
            snew_ref[s, h] = st * gc_ref[h] + _dot_tn(kws, v)
        o = o + cross * dq_ref[h]
        o_ref[:, h * dv:(h + 1) * dv] = _group_norm(o).astype(BF16)


def _ret_sample(q, k, v, state, consts, n_prompt, dec_batch, dec_seq):
    hk = q.shape[1]
    hv = v.shape[1]
    dk, dv = hk // RET_HEADS, hv // RET_HEADS
    seqs = 16 // dec_seq if dec_seq < 16 else 1
    rows = seqs * dec_seq
    base = n_prompt // rows
    dmat, dq, dkk, gc = consts
    row = lambda i: (base + i, 0)
    return pl.pallas_call(
        functools.partial(_ret_sample_kernel, dk=dk, dv=dv, dec_seq=dec_seq, seqs=seqs),
        grid=(dec_batch // seqs,),
        in_specs=[
            pl.BlockSpec((rows, hk), row),
            pl.BlockSpec((rows, hk), row),
            pl.BlockSpec((rows, hv), row),
            pl.BlockSpec((seqs, RET_HEADS, dk, dv), lambda i: (i, 0, 0, 0)),
            _resident(dmat.shape),
            _resident(dq.shape),
            _resident(dkk.shape),
            _resident(gc.shape),
        ],
        out_specs=[
            pl.BlockSpec((rows, hv), lambda i: (i, 0)),
            pl.BlockSpec((seqs, RET_HEADS, dk, dv), lambda i: (i, 0, 0, 0)),
        ],
        out_shape=[
            jax.ShapeDtypeStruct((dec_batch * dec_seq, hv), BF16),
            jax.ShapeDtypeStruct(state.shape, F32),
        ],
        compiler_params=_params(("parallel",), 40),
        name="ret_sample",
    )(q, k, v, state, dmat, dq, dkk, gc)


def _decay_consts(chunk, rows):
    log_gamma = jnp.log1p(-jnp.exp2(-5.0 - jnp.arange(RET_HEADS, dtype=F32)))
    idx = jnp.arange(rows)
    pos = (idx % chunk).astype(F32)
    same = (idx[:, None] // chunk) == (idx[None, :] // chunk)
    diff = pos[:, None] - pos[None, :]
    lg = log_gamma[:, None, None]
    dmat = jnp.where(same & (diff >= 0), jnp.exp(lg * jnp.maximum(diff, 0.0)), 0.0)
    dq = jnp.exp(log_gamma[:, None] * (pos + 1.0))[..., None]
    dk = jnp.exp(log_gamma[:, None] * (chunk - 1.0 - pos))[..., None]
    gc = jnp.exp(log_gamma * chunk)[:, None, None]
    return dmat.astype(F32), dq.astype(F32), dk.astype(F32), gc.astype(F32)


def _mix_ffn_kernel(xp_ref, xs_ref, op_ref, os_ref, gate_ref, gout_ref, wout_ref, gffn_ref, wgu_ref, wdown_ref,
                    h_ref, act_ref, *, d_ff, chunk, n_prompt_tiles):
    o = _pick(op_ref, os_ref, n_prompt_tiles).astype(F32) * gout_ref[...]
    z = (_silu(gate_ref[...].astype(F32)) * o).astype(BF16)
    h1 = _pick(xp_ref, xs_ref, n_prompt_tiles) + _dot(z, wout_ref[...])
    xf = _rms(h1, gffn_ref[...]).astype(BF16)
    for c in range(d_ff // chunk):
        g = _dot(xf, wgu_ref[:, c * chunk:(c + 1) * chunk])
        u = _dot(xf, wgu_ref[:, d_ff + c * chunk:d_ff + (c + 1) * chunk])
        act_ref[:, c * chunk:(c + 1) * chunk] = (_silu(g) * u).astype(BF16)
    h_ref[...] = h1 + _dot(act_ref[...], wdown_ref[...])


def _mix_ffn(x_p, x_s, onorm_p, onorm_s, gate, g_out, w_out, g_ffn, w_gu, w_down):
    d = x_p.shape[1]
    n = x_p.shape[0] + x_s.shape[0]
    hv = onorm_p.shape[1]
    d_ff = w_down.shape[0]
    tm = TOKEN_TILE
    npt = onorm_p.shape[0] // tm
    return pl.pallas_call(
        functools.partial(_mix_ffn_kernel, d_ff=d_ff, chunk=256, n_prompt_tiles=npt),
        grid=(n // tm,),
        in_specs=_split_specs(tm, d, npt) + _split_specs(tm, hv, npt) + [
            pl.BlockSpec((tm, hv), lambda i: (i, 0)),
            _resident((1, hv)),
            _resident(w_out.shape),
            _resident((1, d)),
            _resident(w_gu.shape),
            _resident(w_down.shape),
        ],
        out_specs=pl.BlockSpec((tm, d), lambda i: (i, 0)),
        out_shape=jax.ShapeDtypeStruct((n, d), F32),
        scratch_shapes=[pltpu.VMEM((tm, d_ff), BF16)],
        compiler_params=_params(("parallel",), 56),
        name="mix_ffn",
    )(x_p, x_s, onorm_p, onorm_s, gate, g_out, w_out, g_ffn, w_gu, w_down)


def _rope_lanes(x, cos, sin_signed, half):
    width = x.shape[-1]
    lane = lax.broadcasted_iota(jnp.int32, x.shape, 1)
    first = (lane % (2 * half)) < half
    rot = jnp.where(first, pltpu.roll(x, width - half, axis=1), pltpu.roll(x, half, axis=1))
    return x * cos + rot * sin_signed


def _unit_rms(x, width):
    ms = jnp.sum(x * x, axis=-1, keepdims=True) * (1.0 / width)
    return x * lax.rsqrt(ms + EPS)


HEAD_PAD = 256
V_PAD = 144


def _mla_proj_kernel(h_ref, gmix_ref, gkv_ref, wdq_ref, gqa_ref, wuq_ref, gqn_ref, gqp_ref,
                     wdkvc_ref, wdkvp_ref, gkva_ref, gkpe_ref, wuk_ref, wuv_ref, gkn_ref,
                     cos_ref, sin_ref,
                     qcat_ref, kcat_ref, vt_ref, ckv_ref, kpe_ref, *, scale):
    h = h_ref[...]
    r = lax.rsqrt(jnp.mean(h * h, axis=-1, keepdims=True) + EPS)
    xn = ((h * r) * gmix_ref[...]).astype(BF16)
    xk = ((h * r) * gkv_ref[...]).astype(BF16)
    cos = cos_ref[...]
    sin = sin_ref[...]
    half = MLA_ROPE // 2

    ckv = _rms(_dot(xk, wdkvc_ref[...]), gkva_ref[...])
    ckv_ref[...] = ckv
    ap = _dot(xk, wdkvp_ref[...])
    kpe = _rope_lanes(_unit_rms(ap, MLA_ROPE) * gkpe_ref[...], cos, sin, half)
    kpe_ref[...] = kpe[:, :MLA_ROPE]
    kpe_b = kpe.astype(BF16)

    cq = _rms(_dot(xn, wdq_ref[...]), gqa_ref[...]).astype(BF16)
    q = _dot(cq, wuq_ref[...])
    cb = ckv.astype(BF16)
    kn = _dot(cb, wuk_ref[...])
    for hd in range(MLA_HEADS):
        c0 = hd * HEAD_PAD
        qn = _unit_rms(q[:, c0:c0 + MLA_NOPE], MLA_NOPE) * gqn_ref[...]
        qp = _unit_rms(q[:, c0 + MLA_NOPE:c0 + HEAD_PAD], MLA_ROPE) * gqp_ref[...]
        qp = _rope_lanes(qp, cos, sin, half)
        qcat_ref[:, c0:c0 + MLA_NOPE] = (qn * scale).astype(BF16)
        qcat_ref[:, c0 + MLA_NOPE:c0 + HEAD_PAD] = (qp * scale).astype(BF16)
        kh = _unit_rms(kn[:, hd * MLA_NOPE:(hd + 1) * MLA_NOPE], MLA_NOPE) * gkn_ref[...]
        kcat_ref[:, c0:c0 + MLA_NOPE] = kh.astype(BF16)
        kcat_ref[:, c0 + MLA_NOPE:c0 + HEAD_PAD] = kpe_b
    vt = _dot(cb, wuv_ref[...]).T
    ones = jnp.ones((V_PAD - MLA_DV, vt.shape[1]), BF16)
    for hd in range(MLA_HEADS):
        vt_ref[hd * V_PAD:hd * V_PAD + MLA_DV, :] = vt[hd * MLA_DV:(hd + 1) * MLA_DV, :].astype(BF16)
        vt_ref[hd * V_PAD + MLA_DV:(hd + 1) * V_PAD, :] = ones


def _mla_proj(h, weights, cos_tab, sin_tab, n_prompt_tiles, tiles_per_seq, scale):
    n, d = h.shape
    tm = TOKEN_TILE
    hc = MLA_HEADS * HEAD_PAD
    hn = MLA_HEADS * V_PAD
    kv_lora = weights[7].shape[1]

    def tab_map(i):
        return (jnp.where(i < n_prompt_tiles, i % tiles_per_seq, tiles_per_seq), 0)

    row = lambda i: (i, 0)
    in_specs = [pl.BlockSpec((tm, d), row)] + [_resident(w.shape) for w in weights]
    in_specs += [pl.BlockSpec((tm, LANES), tab_map), pl.BlockSpec((tm, LANES), tab_map)]
    return pl.pallas_call(
        functools.partial(_mla_proj_kernel, scale=scale),
        grid=(n // tm,),
        in_specs=in_specs,
        out_specs=[
            pl.BlockSpec((tm, hc), row),
            pl.BlockSpec((tm, hc), row),
            pl.BlockSpec((hn, tm), lambda i: (0, i)),
            pl.BlockSpec((tm, kv_lora), row),
            pl.BlockSpec((tm, MLA_ROPE), row),
        ],
        out_shape=[
            jax.ShapeDtypeStruct((n, hc), BF16),
            jax.ShapeDtypeStruct((n, hc), BF16),
            jax.ShapeDtypeStruct((hn, n), BF16),
            jax.ShapeDtypeStruct((n, kv_lora), F32),
            jax.ShapeDtypeStruct((n, MLA_ROPE), F32),
        ],
        compiler_params=_params(("parallel",), 48),
        name="mla_proj",
    )(h, *weights, cos_tab, sin_tab)


def _prompt_attn_kernel(qi_ref, ki_ref, q_ref, k_ref, vt_ref, o_ref, *scratch, tile, qblock):
    m_refs = scratch[:MLA_HEADS]
    acc_refs = scratch[MLA_HEADS:]
    qi = qi_ref[pl.program_id(1)]
    ki = ki_ref[pl.program_id(1)]

    @pl.when(ki == 0)
    def _():
        for h in range(MLA_HEADS):
            m_refs[h][...] = jnp.full_like(m_refs[h], -jnp.inf)
            acc_refs[h][...] = jnp.zeros_like(acc_refs[h])

    def key_tile(diagonal):
        if diagonal:
            key = lax.broadcasted_iota(jnp.int32, (qblock, tile), 0)
            qry = lax.broadcasted_iota(jnp.int32, (qblock, tile), 1)
        n_kb = tile // qblock
        for h in range(MLA_HEADS):
            cols = slice(h * HEAD_PAD, (h + 1) * HEAD_PAD)
            parts = []
            for j in range(n_kb):
                keys = slice(j * qblock, (j + 1) * qblock)
                s = _dot_nt(k_ref[keys, cols], q_ref[:, cols])
                if diagonal:
                    s = jnp.where(key + j * qblock <= qry, s, -jnp.inf)
                parts.append(s)
            m_old = m_refs[h][...]
            m_new = m_old
            for s in parts:
                m_new = jnp.maximum(m_new, jnp.max(s, axis=0, keepdims=True))
            m_refs[h][...] = m_new
            acc = jnp.exp2(m_old - m_new) * acc_refs[h][...]
            for j, s in enumerate(parts):
                p = jnp.exp2(s - m_new).astype(BF16)
                acc = acc + _dot(vt_ref[h * V_PAD:(h + 1) * V_PAD, j * qblock:(j + 1) * qblock], p)
            acc_refs[h][...] = acc

    @pl.when(ki < qi)
    def _():
        key_tile(False)

    @pl.when(ki == qi)
    def _():
        key_tile(True)
        outs = []
        for h in range(MLA_HEADS):
            blk = acc_refs[h][...]
            outs.append(blk[:MLA_DV] / blk[MLA_DV:MLA_DV + 1])
        o_ref[...] = jnp.concatenate(outs, axis=0).T


def _prompt_attn(qcat, kcat, vt, batch, seq):
    t = ATTN_TILE
    per_seq = seq // t
    hc = qcat.shape[1]
    hn = MLA_HEADS * MLA_DV
    hvp = vt.shape[0]
    pairs = [(qi, ki) for qi in range(per_seq) for ki in range(qi + 1)]
    qi_tab = jnp.asarray([p[0] for p in pairs], jnp.int32)
    ki_tab = jnp.asarray([p[1] for p in pairs], jnp.int32)
    qrow = lambda b, s, qt, kt: (b * per_seq + qt[s], 0)
    krow = lambda b, s, qt, kt: (b * per_seq + kt[s], 0)
    vcol = lambda b, s, qt, kt: (0, b * per_seq + kt[s])
    grid_spec = pltpu.PrefetchScalarGridSpec(
        num_scalar_prefetch=2,
        grid=(batch, len(pairs)),
        in_specs=[
            pl.BlockSpec((t, hc), qrow),
            pl.BlockSpec((t, hc), krow),
            pl.BlockSpec((hvp, t), vcol),
        ],
        out_specs=pl.BlockSpec((t, hn), qrow),
        scratch_shapes=([pltpu.VMEM((1, t), F32)] * MLA_HEADS + [pltpu.VMEM((V_PAD, t), F32)] * MLA_HEADS),
    )
    return pl.pallas_call(
        functools.partial(_prompt_attn_kernel, tile=t, qblock=ATTN_QBLOCK),
        grid_spec=grid_spec,
        out_shape=jax.ShapeDtypeStruct((batch * seq, hn), F32),
        compiler_params=_params(("parallel", "arbitrary"), 40),
        name="prompt_attn",
    )(qi_tab, ki_tab, qcat, kcat, vt)


def _absorb_q_kernel(qcat_ref, gkn_ref, wuk_ref, qa_ref, qps_ref):
    lora = wuk_ref.shape[0]
    for h in range(MLA_HEADS):
        c0 = h * HEAD_PAD
        qg = (qcat_ref[:, c0:c0 + MLA_NOPE].astype(F32) * gkn_ref[...]).astype(BF16)
        qa_ref[:, h * lora:(h + 1) * lora] = _dot_nt(qg, wuk_ref[:, h * MLA_NOPE:(h + 1) * MLA_NOPE])
        qps_ref[:, h * LANES:(h + 1) * LANES] = qcat_ref[:, c0 + MLA_NOPE:c0 + HEAD_PAD].astype(F32)


def _absorb_q(qcat, gkn, wuk, n_prompt, n_sample):
    hc = qcat.shape[1]
    lora = wuk.shape[0]
    tm = min(256, n_sample)
    base = n_prompt // tm
    return pl.pallas_call(
        _absorb_q_kernel,
        grid=(n_sample // tm,),
        in_specs=[
            pl.BlockSpec((tm, hc), lambda i: (base + i, 0)),
            _resident(gkn.shape),
            _resident(wuk.shape),
        ],
        out_specs=[
            pl.BlockSpec((tm, MLA_HEADS * lora), lambda i: (i, 0)),
            pl.BlockSpec((tm, MLA_HEADS * LANES), lambda i: (i, 0)),
        ],
        out_shape=[
            jax.ShapeDtypeStruct((n_sample, MLA_HEADS * lora), F32),
            jax.ShapeDtypeStruct((n_sample, MLA_HEADS * LANES), F32),
        ],
        compiler_params=_params(("parallel",), 32),
        name="absorb_q",
    )(qcat, gkn, wuk)


def _sample_attn_kernel(pt_ref, qa_ref, qp_ref, cnew_ref, pnew_ref, wukt_ref, wuv_ref, cache_c_ref, cache_pt_ref,
                        o_ref, cbuf, pbuf, sem, m_ref, l_ref, acc_ref, lhs_s, qa_s, qp_s, cnew_s, pnew_s,
                        cbb0, cbb1, s_scr0, s_scr1, *, n_pages, ppc, page, dec_seq, lora, key_block):
    b = pl.program_id(0)
    nb = pl.num_programs(0)
    n_chunks = n_pages // ppc
    hq = MLA_HEADS * dec_seq
    n_proj = MLA_HEADS * MLA_NOPE

    def copies(seq, chunk, slot):
        out = []
        for p in range(ppc):
            pg = pt_ref[seq, chunk * ppc + p]
            out.append(pltpu.make_async_copy(cache_c_ref.at[pg], cbuf.at[slot, pl.ds(p * page, page)], sem.at[0, slot]))
            out.append(pltpu.make_async_copy(cache_pt_ref.at[pg], pbuf.at[slot, :, pl.ds(p * page, page)],
                                             sem.at[1, slot]))
        return out

    @pl.when(b == 0)
    def _():
        for cp in copies(0, 0, 0):
            cp.start()
        lhs_s[0:n_proj, :] = wukt_ref[...]

    for h in range(MLA_HEADS):
        qa_s[h * dec_seq:(h + 1) * dec_seq, :] = qa_ref[:, h * lora:(h + 1) * lora]
        qp_s[h * dec_seq:(h + 1) * dec_seq, :] = qp_ref[:, h * LANES:(h + 1) * LANES]
    lhs_s[n_proj:n_proj + hq, :] = qa_s[...].astype(BF16)
    qp = qp_s[...][:, :MLA_ROPE].astype(BF16)
    m_ref[...] = jnp.full_like(m_ref, -jnp.inf)
    l_ref[...] = jnp.zeros_like(l_ref)
    acc_ref[...] = jnp.zeros_like(acc_ref)

    def latent_scores(cb):
        t = cb.shape[0]
        kb = min(key_block, t)
        blocks = []
        for j in range(t // kb):
            kt = _dot_nt(lhs_s[...], cb[j * kb:(j + 1) * kb, :])
            parts = []
            for h in range(MLA_HEADS):
                kh = kt[h * MLA_NOPE:(h + 1) * MLA_NOPE, :]
                ms = jnp.sum(kh * kh, axis=0, keepdims=True) * (1.0 / MLA_NOPE)
                parts.append(kt[n_proj + h * dec_seq:n_proj + (h + 1) * dec_seq, :] * lax.rsqrt(ms + EPS))
            blocks.append(jnp.concatenate(parts, axis=0))
        return blocks[0] if len(blocks) == 1 else jnp.concatenate(blocks, axis=1)

    cbb = (cbb0, cbb1)
    s_scr = (s_scr0, s_scr1)

    def project(slot, par):
        cb = cbuf[slot].astype(BF16)
        cbb[par][...] = cb
        s_scr[par][...] = latent_scores(cb) + _dot(qp, pbuf[slot].astype(BF16))

    def absorb(s, cb):
        m_old = m_ref[...]
        m_new = jnp.maximum(m_old, jnp.max(s, axis=-1, keepdims=True))
        alpha = jnp.exp2(m_old - m_new)
        p = jnp.exp2(s - m_new)
        l_ref[...] = alpha * l_ref[...] + jnp.sum(p, axis=-1, keepdims=True)
        m_ref[...] = m_new
        acc_ref[...] = alpha * acc_ref[...] + _dot(p.astype(BF16), cb)

    def fetch_next(c, slot):
        if c + 1 < n_chunks:
            for cp in copies(b, c + 1, 1 - slot):
                cp.start()
        else:
            @pl.when(b + 1 < nb)
            def _():
                for cp in copies(b + 1, 0, 1 - slot):
                    cp.start()

        for cp in copies(b, c, slot):
            cp.wait()

    for c in range(n_chunks):
        slot = (b * n_chunks + c) % 2 if n_chunks % 2 else c % 2
        fetch_next(c, slot)
        project(slot, c % 2)
        if c:
            absorb(s_scr[(c - 1) % 2][...], cbb[(c - 1) % 2][...])

    cnew_s[...] = jnp.zeros_like(cnew_s)
    pnew_s[...] = jnp.zeros_like(pnew_s)
    cnew_s[0:dec_seq, :] = cnew_ref[...]
    pnew_s[0:dec_seq, :] = pnew_ref[...]
    cnew = cnew_s[...].astype(BF16)
    row_tok = lax.broadcasted_iota(jnp.int32, (hq, LANES), 0) % dec_seq
    col_tok = lax.broadcasted_iota(jnp.int32, (hq, LANES), 1)
    s_new = latent_scores(cnew) + _dot_nt(qp, pnew_s[...].astype(BF16))
    s_new = jnp.where(col_tok <= row_tok, s_new, -jnp.inf)
    last = (n_chunks - 1) % 2
    absorb(jnp.concatenate([s_scr[last][...], s_new], axis=1), jnp.concatenate([cbb[last][...], cnew], axis=0))

    lat = (acc_ref[...] / l_ref[...]).astype(BF16)
    full = _dot(lat, wuv_ref[...])
    for h in range(MLA_HEADS):
        cols = slice(h * MLA_DV, (h + 1) * MLA_DV)
        o_ref[:, cols] = full[h * dec_seq:(h + 1) * dec_seq, cols]


def _sample_attn(page_table, qa, qp, ckv, kpe, wukt, wuv, cache_c, cache_pt, n_prompt, dec_batch, dec_seq):
    n_pages = page_table.shape[1]
    page = cache_c.shape[1]
    lora = cache_c.shape[2]
    ppc = min(PAGES_PER_CHUNK, n_pages)
    hn = MLA_HEADS * MLA_DV
    hq = MLA_HEADS * dec_seq
    base = n_prompt // dec_seq
    srow = lambda b, pt: (b, 0)
    nrow = lambda b, pt: (base + b, 0)
    grid_spec = pltpu.PrefetchScalarGridSpec(
        num_scalar_prefetch=1,
        grid=(dec_batch,),
        in_specs=[
            pl.BlockSpec((dec_seq, MLA_HEADS * lora), srow),
            pl.BlockSpec((dec_seq, MLA_HEADS * LANES), srow),
            pl.BlockSpec((dec_seq, lora), nrow),
            pl.BlockSpec((dec_seq, MLA_ROPE), nrow),
            pl.BlockSpec(wukt.shape, lambda b, pt: (0, 0), pipeline_mode=pl.Buffered(1)),
            pl.BlockSpec(wuv.shape, lambda b, pt: (0, 0), pipeline_mode=pl.Buffered(1)),
            pl.BlockSpec(memory_space=pl.ANY),
            pl.BlockSpec(memory_space=pl.ANY),
        ],
        out_specs=pl.BlockSpec((dec_seq, hn), srow),
        scratch_shapes=[
            pltpu.VMEM((2, ppc * page, lora), F32),
            pltpu.VMEM((2, MLA_ROPE, ppc * page), F32),
            pltpu.SemaphoreType.DMA((2, 2)),
            pltpu.VMEM((hq, 1), F32),
            pltpu.VMEM((hq, 1), F32),
            pltpu.VMEM((hq, lora), F32),
            pltpu.VMEM((MLA_HEADS * MLA_NOPE + hq, lora), BF16),
            pltpu.VMEM((hq, lora), F32),
            pltpu.VMEM((hq, LANES), F32),
            pltpu.VMEM((LANES, lora), F32),
            pltpu.VMEM((LANES, MLA_ROPE), F32),
            pltpu.VMEM((ppc * page, lora), BF16),
            pltpu.VMEM((ppc * page, lora), BF16),
            pltpu.VMEM((hq, ppc * page), F32),
            pltpu.VMEM((hq, ppc * page), F32),
        ],
    )
    return pl.pallas_call(
        functools.partial(_sample_attn_kernel, n_pages=n_pages, ppc=ppc, page=page, dec_seq=dec_seq, lora=lora,
                          key_block=SAMPLE_KEY_BLOCK),
        grid_spec=grid_spec,
        out_shape=jax.ShapeDtypeStruct((dec_batch * dec_seq, hn), F32),
        compiler_params=_params(("arbitrary",), 40),
        name="sample_attn",
    )(page_table, qa, qp, ckv, kpe, wukt, wuv, cache_c, cache_pt)


def _route_kernel(h_ref, op_ref, os_ref, wo_ref, gffn_ref, wr_ref, tri_ref, h3_ref, xf_ref, info_ref, infot_ref,
                  cnt_ref, carry_ref, *, n_prompt_tiles):
    i = pl.program_id(0)

    @pl.when(i == 0)
    def _():
        carry_ref[...] = jnp.zeros_like(carry_ref)

    o = jnp.where(i < n_prompt_tiles, op_ref[...], os_ref[...])
    h3 = h_ref[...] + _dot(o.astype(BF16), wo_ref[...])
    h3_ref[...] = h3
    xf = _rms(h3, gffn_ref[...])
    xf_ref[...] = xf
    logits = _dot(xf.astype(BF16), wr_ref[...])
    lane = lax.broadcasted_iota(jnp.int32, logits.shape, 1).astype(F32)
    lg = jnp.where(lane < N_EXPERTS, logits, -jnp.inf)
    v1 = jnp.max(lg, axis=-1, keepdims=True)
    i1 = jnp.min(jnp.where(lg == v1, lane, float(LANES)), axis=-1, keepdims=True)
    lg2 = jnp.where(lane == i1, -jnp.inf, lg)
    v2 = jnp.max(lg2, axis=-1, keepdims=True)
    i2 = jnp.min(jnp.where(lg2 == v2, lane, float(LANES)), axis=-1, keepdims=True)
    e = jnp.exp(v2 - v1)
    g1 = 1.0 / (1.0 + e)
    g2 = e / (1.0 + e)
    oh1 = lane == i1
    oh2 = lane == i2
    chosen = jnp.where(oh1 | oh2, 1.0, 0.0)
    cum = _dot(tri_ref[...], chosen.astype(BF16))
    before = cum - chosen + carry_ref[...]
    pos1 = jnp.sum(jnp.where(oh1, before, 0.0), axis=-1, keepdims=True)
    pos2 = jnp.sum(jnp.where(oh2, before, 0.0), axis=-1, keepdims=True)
    carry_ref[...] = carry_ref[...] + cum[cum.shape[0] - 1:, :]
    cnt_ref[...] = carry_ref[...]
    info = jnp.where(lane == 0, i1, 0.0)
    info = jnp.where(lane == 1, i2, info)
    info = jnp.where(lane == 2, pos1, info)
    info = jnp.where(lane == 3, pos2, info)
    info = jnp.where(lane == 4, g1, info)
    info = jnp.where(lane == 5, g2, info)
    info_ref[...] = info
    infot_ref[...] = info.T[:8, :]


def _route(h2, o_attn_p, o_attn_s, w_o, g_ffn, w_r, tri):
    n, d = h2.shape
    tm = TOKEN_TILE
    npt = o_attn_p.shape[0] // tm
    hn = o_attn_p.shape[1]
    row = lambda i: (i, 0)
    return pl.pallas_call(
        functools.partial(_route_kernel, n_prompt_tiles=npt),
        grid=(n // tm,),
        in_specs=[
            pl.BlockSpec((tm, d), row),
            pl.BlockSpec((tm, hn), lambda i: (jnp.minimum(i, npt - 1), 0)),
            pl.BlockSpec((tm, hn), lambda i: (jnp.maximum(i - npt, 0), 0)),
            _resident(w_o.shape),
            _resident((1, d)),
            _resident(w_r.shape),
            _resident(tri.shape),
        ],
        out_specs=[
            pl.BlockSpec((tm, d), row),
            pl.BlockSpec((tm, d), row),
            pl.BlockSpec((tm, LANES), row),
            pl.BlockSpec((8, tm), lambda i: (0, i)),
            pl.BlockSpec((1, LANES), lambda i: (0, 0)),
        ],
        out_shape=[
            jax.ShapeDtypeStruct((n, d), F32),
            jax.ShapeDtypeStruct((n, d), F32),
            jax.ShapeDtypeStruct((n, LANES), F32),
            jax.ShapeDtypeStruct((8, n), F32),
            jax.ShapeDtypeStruct((1, LANES), F32),
        ],
        scratch_shapes=[pltpu.VMEM((1, LANES), F32)],
        compiler_params=_params(("arbitrary",), 40),
        name="route",
    )(h2, o_attn_p, o_attn_s, w_o, g_ffn, w_r, tri)


def _scatter_kernel(dest_ref, x_ref, xs_in_ref, xs_ref, sem, *, tm):
    del xs_in_ref

    def issue(r, carry):
        for k in range(2):
            d = dest_ref[0, 0, k * tm + r]
            pltpu.make_async_copy(x_ref.at[pl.ds(r, 1)], xs_ref.at[pl.ds(d, 1)], sem).start(priority=k)
        return carry

    lax.fori_loop(0, tm, issue, 0, unroll=8)

    def drain(r, carry):
        for k in range(2):
            pltpu.make_async_copy(x_ref.at[pl.ds(r, 1)], xs_ref.at[pl.ds(0, 1)], sem).wait()
        return carry

    lax.fori_loop(0, tm, drain, 0, unroll=8)


def _scatter_rows(dest, xf, xs0):
    n, d = xf.shape
    tm = TOKEN_TILE
    return pl.pallas_call(
        functools.partial(_scatter_kernel, tm=tm),
        grid=(n // tm,),
        in_specs=[
            pl.BlockSpec((1, 1, 2 * tm), lambda i: (i, 0, 0), memory_space=pltpu.SMEM),
            pl.BlockSpec((tm, d), lambda i: (i, 0)),
            pl.BlockSpec(memory_space=pl.ANY),
        ],
        out_specs=pl.BlockSpec(memory_space=pl.ANY),
        out_shape=jax.ShapeDtypeStruct(xs0.shape, xs0.dtype),
        scratch_shapes=[pltpu.SemaphoreType.DMA(())],
        input_output_aliases={2: 0},
        compiler_params=_params(("arbitrary",), 32),
        name="scatter_rows",
    )(dest, xf, xs0)


def _experts_kernel(te_ref, tr_ref, x_ref, wg_ref, wu_ref, wd_ref, o_ref, acc_ref, xb_ref, *, sub):
    t = pl.program_id(0)
    c = pl.program_id(1)
    rows = tr_ref[t]
    n_sub = x_ref.shape[0] // sub

    @pl.when(c == 0)
    def _():
        xb_ref[...] = x_ref[...].astype(BF16)
        acc_ref[...] = jnp.zeros_like(acc_ref)

    for sb in range(n_sub):
        r = slice(sb * sub, (sb + 1) * sub)

        @pl.when(sb * sub < rows)
        def _():
            xb = xb_ref[r, :]
            act = (_silu(_dot(xb, wg_ref[0])) * _dot(xb, wu_ref[0])).astype(BF16)
            acc_ref[r, :] += _dot(act, wd_ref[0])

    @pl.when(c == pl.num_programs(1) - 1)
    def _():
        o_ref[...] = acc_ref[...]


def _experts(tile_expert, tile_rows, xs, w_gu, w_down):
    rows, d = xs.shape
    tm = EXPERT_TILE
    fc = FF_CHUNK
    d_ff = w_down.shape[1]
    n_c = d_ff // fc
    n_tiles = rows // tm

    def ceff(t, c, tr):
        return jnp.where(tr[t] > 0, c, n_c - 1)

    grid_spec = pltpu.PrefetchScalarGridSpec(
        num_scalar_prefetch=2,
        grid=(n_tiles, n_c),
        in_specs=[
            pl.BlockSpec((tm, d), lambda t, c, te, tv: (t, 0)),
            pl.BlockSpec((1, d, fc), lambda t, c, te, tv: (te[t], 0, ceff(t, c, tv))),
            pl.BlockSpec((1, d, fc), lambda t, c, te, tv: (te[t], 0, n_c + ceff(t, c, tv))),
            pl.BlockSpec((1, fc, d), lambda t, c, te, tv: (te[t], ceff(t, c, tv), 0)),
        ],
        out_specs=pl.BlockSpec((tm, d), lambda t, c, te, tv: (t, 0)),
        scratch_shapes=[pltpu.VMEM((tm, d), F32), pltpu.VMEM((tm, d), BF16)],
    )
    return pl.pallas_call(
        functools.partial(_experts_kernel, sub=EXPERT_SUB),
        grid_spec=grid_spec,
        out_shape=jax.ShapeDtypeStruct((rows, d), F32),
        compiler_params=_params(("arbitrary", "arbitrary"), 52),
        name="experts",
    )(tile_expert, tile_rows, xs, w_gu, w_gu, w_down)


def _combine_kernel(dest_ref, h_ref, info_ref, ys_ref, yp_ref, ysm_ref, buf, sem, *, tm, n_prompt_tiles):
    def issue(r, carry):
        for k in range(2):
            d = dest_ref[0, 0, k * tm + r]
            pltpu.make_async_copy(ys_ref.at[pl.ds(d, 1)], buf.at[k, pl.ds(r, 1)], sem).start(priority=k)
        return carry

    lax.fori_loop(0, tm, issue, 0, unroll=8)

    def drain(r, carry):
        for k in range(2):
            pltpu.make_async_copy(ys_ref.at[pl.ds(0, 1)], buf.at[k, pl.ds(r, 1)], sem).wait()
        return carry

    lax.fori_loop(0, tm, drain, 0, unroll=8)
    info = info_ref[...]
    y = h_ref[...] + info[:, 4:5] * buf[0] + info[:, 5:6] * buf[1]

    @pl.when(pl.program_id(0) < n_prompt_tiles)
    def _():
        yp_ref[...] = y

    @pl.when(pl.program_id(0) >= n_prompt_tiles)
    def _():
        ysm_ref[...] = y


def _combine(dest, h3, info, ys, n_prompt):
    n, d = h3.shape
    tm = TOKEN_TILE
    npt = n_prompt // tm
    return pl.pallas_call(
        functools.partial(_combine_kernel, tm=tm, n_prompt_tiles=npt),
        grid=(n // tm,),
        in_specs=[
            pl.BlockSpec((1, 1, 2 * tm), lambda i: (i, 0, 0), memory_space=pltpu.SMEM),
            pl.BlockSpec((tm, d), lambda i: (i, 0)),
            pl.BlockSpec((tm, LANES), lambda i: (i, 0)),
            pl.BlockSpec(memory_space=pl.ANY),
        ],
        out_specs=[
            pl.BlockSpec((tm, d), lambda i: (jnp.minimum(i, npt - 1), 0)),
            pl.BlockSpec((tm, d), lambda i: (jnp.maximum(i - npt, 0), 0)),
        ],
        out_shape=[
            jax.ShapeDtypeStruct((n_prompt, d), F32),
            jax.ShapeDtypeStruct((n - n_prompt, d), F32),
        ],
        scratch_shapes=[pltpu.VMEM((2, tm, d), F32), pltpu.SemaphoreType.DMA(())],
        compiler_params=_params(("arbitrary",), 32),
        name="combine",
    )(dest, h3, info, ys)


def _rope_tables(pos, dim):
    inv = ROPE_BASE ** (-jnp.arange(0, dim, 2, dtype=F32) / dim)
    ang = pos.astype(F32)[:, None] * inv[None, :]
    return jnp.cos(ang), jnp.sin(ang)


def kernel(x_prompt, x_sample, state_ret, cache_ckv, cache_kpe, page_table, g_pre_mix, g_pre_ffn, ret_w_in, ret_g_out, ret_w_out, mla_g_kv_in, mla_w_dkv, mla_g_kv_a, mla_g_kpe, mla_w_uk, mla_w_uv, mla_g_kn, mla_w_dq, mla_g_qa, mla_w_uq, mla_g_qn, mla_g_qp, mla_w_o, ffn_w_gu, ffn_w_down, moe_w_router, moe_w_gu, moe_w_down):
    batch, seq, d = x_prompt.shape
    dec_batch, dec_seq, _ = x_sample.shape
    n_prompt = batch * seq
    n_sample = dec_batch * dec_seq
    n = n_prompt + n_sample
    tm = TOKEN_TILE
    past = page_table.shape[1] * cache_ckv.shape[1]
    hv = ret_g_out.shape[1]
    hk = (ret_w_in.shape[2] - 2 * hv) // 2
    kv_lora = mla_g_kv_a.shape[0]
    assert seq % tm == 0 and n_sample % tm == 0 and seq % ATTN_TILE == 0 and seq % RET_ROWS == 0
    n_prompt_tiles = n_prompt // tm
    tiles_per_seq = seq // tm

    x_p = x_prompt.reshape(n_prompt, d)
    x_s = x_sample.reshape(n_sample, d)

    pos = jnp.concatenate([jnp.arange(seq), past + (jnp.arange(tm) % dec_seq)])
    cos_r, sin_r = _rope_tables(pos, hk // RET_HEADS)
    cos_m, sin_m = _rope_tables(pos, MLA_ROPE)
    cos_m = jnp.tile(cos_m, (1, 2 * LANES // MLA_ROPE))
    sin_m = jnp.tile(jnp.concatenate([-sin_m, sin_m], axis=-1), (1, LANES // MLA_ROPE))

    q, k, v, gate = _ret_in(x_p, x_s, g_pre_mix[0][None], ret_w_in[0].astype(BF16), cos_r, sin_r,
                            n_prompt_tiles, tiles_per_seq, hk, hv)
    onorm_p, s_prompt = _ret_prompt(q, k, v, _decay_consts(RET_CHUNK, RET_CHUNK), batch, seq)
    seqs = 16 // dec_seq if dec_seq < 16 else 1
    onorm_s, s_sample = _ret_sample(q, k, v, state_ret[0], _decay_consts(dec_seq, seqs * dec_seq),
                                    n_prompt, dec_batch, dec_seq)
    h2 = _mix_ffn(x_p, x_s, onorm_p, onorm_s, gate, ret_g_out[0][None], ret_w_out[0].astype(BF16),
                  g_pre_ffn[0][None], ffn_w_gu[0].astype(BF16), ffn_w_down[0].astype(BF16))

    q_lora = mla_w_dq.shape[2]
    head_dim = MLA_NOPE + MLA_ROPE
    w_uq = jnp.pad(mla_w_uq[0].reshape(q_lora, MLA_HEADS, head_dim), ((0, 0), (0, 0), (0, HEAD_PAD - head_dim)))
    w_uq = w_uq.reshape(q_lora, MLA_HEADS * HEAD_PAD).astype(BF16)
    w_dkv_c = mla_w_dkv[:, :kv_lora].astype(BF16)
    w_dkv_p = jnp.pad(mla_w_dkv[:, kv_lora:], ((0, 0), (0, LANES - MLA_ROPE))).astype(BF16)
    pad_rope = lambda g: jnp.pad(g, (0, LANES - MLA_ROPE))[None]
    weights = [
        g_pre_mix[1][None], mla_g_kv_in[None], mla_w_dq[0].astype(BF16), mla_g_qa[0][None], w_uq,
        mla_g_qn[0][None], pad_rope(mla_g_qp[0]),
        w_dkv_c, w_dkv_p, mla_g_kv_a[None], pad_rope(mla_g_kpe),
        mla_w_uk.astype(BF16), mla_w_uv.astype(BF16), mla_g_kn[None],
    ]
    scale = float(head_dim ** -0.5 * 1.4426950408889634)
    qcat, kcat, vt, ckv, kpe = _mla_proj(h2, weights, cos_m, sin_m, n_prompt_tiles, tiles_per_seq, scale)
    o_attn_p = _prompt_attn(qcat, kcat, vt, batch, seq)
    qa, qps = _absorb_q(qcat, mla_g_kn[None], mla_w_uk.astype(BF16), n_prompt, n_sample)
    o_attn_s = _sample_attn(page_table, qa, qps, ckv, kpe, mla_w_uk.T.astype(BF16), mla_w_uv.astype(BF16),
                            cache_ckv, cache_kpe.transpose(0, 2, 1), n_prompt, dec_batch, dec_seq)

    w_r = jnp.pad(moe_w_router[0], ((0, 0), (0, LANES - N_EXPERTS))).astype(BF16)
    tri = (jnp.arange(tm)[:, None] >= jnp.arange(tm)[None, :]).astype(BF16)
    h3, xf, info, info_t, counts = _route(h2, o_attn_p, o_attn_s, mla_w_o[0].astype(BF16), g_pre_ffn[1][None],
                                          w_r, tri)

    te = EXPERT_TILE
    counts = counts[0, :N_EXPERTS].astype(jnp.int32)
    tiles_e = (counts + te - 1) // te
    tile_end = jnp.cumsum(tiles_e)
    offsets = (tile_end - tiles_e) * te
    n_tiles = (2 * n) // te + N_EXPERTS
    tile_ids = jnp.arange(n_tiles, dtype=jnp.int32)
    tile_valid = tile_ids < tile_end[-1]
    tile_expert = jnp.minimum(jnp.sum(tile_ids[:, None] >= tile_end[None, :], axis=1), N_EXPERTS - 1)
    first_tile = jnp.zeros_like(tile_ids)
    count_of = jnp.zeros_like(tile_ids)
    for e in range(N_EXPERTS):
        first_tile = jnp.where(tile_expert == e, tile_end[e] - tiles_e[e], first_tile)
        count_of = jnp.where(tile_expert == e, counts[e], count_of)
    tile_rows = jnp.where(tile_valid, jnp.clip(count_of - (tile_ids - first_tile) * te, 0, te), 0).astype(jnp.int32)
    last_expert = jnp.sum((tile_end[-1] - 1) >= tile_end).astype(jnp.int32)
    tile_expert = jnp.where(tile_valid, tile_expert, last_expert).astype(jnp.int32)
    idx = info_t[0:2].astype(jnp.int32)
    base = jnp.zeros_like(idx)
    for e in range(N_EXPERTS):
        base = jnp.where(idx == e, offsets[e], base)
    dest = base + info_t[2:4].astype(jnp.int32)
    dest = dest.reshape(2, n // tm, tm).transpose(1, 0, 2).reshape(n // tm, 1, 2 * tm)

    xs = _scatter_rows(dest, xf, jnp.zeros((n_tiles * te, d), F32))
    ys = _experts(tile_expert, tile_rows, xs, moe_w_gu[0].astype(BF16), moe_w_down[0].astype(BF16))
    y_p, y_s = _combine(dest, h3, info, ys, n_prompt)

    return (
        y_p.reshape(batch, seq, d),
        y_s.reshape(dec_batch, dec_seq, d),
        s_prompt[None],
        s_sample[None],
        ckv[:n_prompt].reshape(batch, seq, kv_lora),
        kpe[:n_prompt].reshape(batch, seq, MLA_ROPE),
        ckv[n_prompt:].reshape(dec_batch, dec_seq, kv_lora),
        kpe[n_prompt:].reshape(dec_batch, dec_seq, MLA_ROPE),
    )
```

```python
import functools

import jax
import jax.numpy as jnp
from jax import lax
from jax.experimental import pallas as pl
from jax.experimental.pallas import tpu as pltpu

F32 = jnp.float32
BF16 = jnp.bfloat16
EPS = 1e-6
ROPE_BASE = 10000.0

RET_HEADS = 4
RET_CHUNK = 128
MLA_HEADS = 8
MLA_NOPE = 128
MLA_ROPE = 64
MLA_DV = 128
N_EXPERTS = 8
LANES = 128

TOKEN_TILE = 512
RET_ROWS = 512
ATTN_TILE = 512
ATTN_QBLOCK = 256
EXPERT_TILE = 1024
EXPERT_SUB = 1024
FF_CHUNK = 512
PAGES_PER_CHUNK = 32
SAMPLE_KEY_BLOCK = 256
MIB = 1024 * 1024


def _params(semantics, vmem_mib, flags=None):
    return pltpu.CompilerParams(dimension_semantics=semantics, vmem_limit_bytes=vmem_mib * MIB, flags=flags)


def _resident(shape):
    return pl.BlockSpec(shape, lambda *_: (0,) * len(shape), pipeline_mode=pl.Buffered(1))


def _rms(x, g):
    ms = jnp.mean(x * x, axis=-1, keepdims=True)
    return (x * lax.rsqrt(ms + EPS)) * g


def _dot(a, b):
    return jnp.dot(a, b, preferred_element_type=F32)


def _dot_nt(a, b):
    return lax.dot_general(a, b, (((1,), (1,)), ((), ())), preferred_element_type=F32)


def _dot_tn(a, b):
    return lax.dot_general(a, b, (((0,), (0,)), ((), ())), preferred_element_type=F32)


def _silu(x):
    return x * jax.nn.sigmoid(x)


def _split_specs(tm, width, n_prompt_tiles):
    return [
        pl.BlockSpec((tm, width), lambda i: (jnp.minimum(i, n_prompt_tiles - 1), 0)),
        pl.BlockSpec((tm, width), lambda i: (jnp.maximum(i - n_prompt_tiles, 0), 0), pipeline_mode=pl.Buffered(1)),
    ]


def _pick(p_ref, s_ref, n_prompt_tiles):
    return jnp.where(pl.program_id(0) < n_prompt_tiles, p_ref[...], s_ref[...])


def _ret_in_kernel(xp_ref, xs_ref, g_ref, w_ref, cos_ref, sin_ref, q_ref, k_ref, v_ref, gate_ref,
                   *, hk, hv, dk, n_prompt_tiles):
    xn = _rms(_pick(xp_ref, xs_ref, n_prompt_tiles), g_ref[...]).astype(BF16)
    cos = cos_ref[...]
    sin = sin_ref[...]
    half = dk // 2

    def rope_store(dst, col, acc, scale):
        x1 = acc[:, :half]
        x2 = acc[:, half:]
        dst[:, col:col + half] = ((x1 * cos - x2 * sin) * scale).astype(BF16)
        dst[:, col + half:col + dk] = ((x1 * sin + x2 * cos) * scale).astype(BF16)

    for h in range(hk // dk):
        rope_store(q_ref, h * dk, _dot(xn, w_ref[:, h * dk:(h + 1) * dk]), 1.0)
    for h in range(hk // dk):
        rope_store(k_ref, h * dk, _dot(xn, w_ref[:, hk + h * dk:hk + (h + 1) * dk]), dk ** -0.5)
    step = 512
    for c in range(hv // step):
        v_ref[:, c * step:(c + 1) * step] = _dot(xn, w_ref[:, 2 * hk + c * step:2 * hk + (c + 1) * step]).astype(BF16)
    for c in range(hv // step):
        gate_ref[:, c * step:(c + 1) * step] = _dot(
            xn, w_ref[:, 2 * hk + hv + c * step:2 * hk + hv + (c + 1) * step]).astype(BF16)


def _ret_in(x_p, x_s, g, w, cos_tab, sin_tab, n_prompt_tiles, tiles_per_seq, hk, hv):
    d = x_p.shape[1]
    n = x_p.shape[0] + x_s.shape[0]
    tm = TOKEN_TILE
    dk = hk // RET_HEADS

    def tab_map(i):
        return (jnp.where(i < n_prompt_tiles, i % tiles_per_seq, tiles_per_seq), 0)

    return pl.pallas_call(
        functools.partial(_ret_in_kernel, hk=hk, hv=hv, dk=dk, n_prompt_tiles=n_prompt_tiles),
        grid=(n // tm,),
        in_specs=_split_specs(tm, d, n_prompt_tiles) + [
            _resident((1, d)),
            _resident(w.shape),
            pl.BlockSpec((tm, dk // 2), tab_map),
            pl.BlockSpec((tm, dk // 2), tab_map),
        ],
        out_specs=[
            pl.BlockSpec((tm, hk), lambda i: (i, 0)),
            pl.BlockSpec((tm, hk), lambda i: (i, 0)),
            pl.BlockSpec((tm, hv), lambda i: (i, 0)),
            pl.BlockSpec((tm, hv), lambda i: (i, 0)),
        ],
        out_shape=[
            jax.ShapeDtypeStruct((n, hk), BF16),
            jax.ShapeDtypeStruct((n, hk), BF16),
            jax.ShapeDtypeStruct((n, hv), BF16),
            jax.ShapeDtypeStruct((n, hv), BF16),
        ],
        compiler_params=_params(("parallel",), 48),
        name="ret_in",
    )(x_p, x_s, g, w, cos_tab, sin_tab)


def _group_norm(o):
    mu = jnp.mean(o, axis=-1, keepdims=True)
    var = jnp.mean(jnp.square(o - mu), axis=-1, keepdims=True)
    return (o - mu) * lax.rsqrt(var + EPS)


def _ret_prompt_kernel(q_ref, k_ref, v_ref, dmat_ref, dq_ref, dk_ref, gc_ref, o_ref, sfin_ref, *state_refs,
                       dk, dv, chunk):
    j = pl.program_id(1)

    @pl.when(j == 0)
    def _():
        for h in range(RET_HEADS):
            state_refs[h][...] = jnp.zeros_like(state_refs[h])

    def one_chunk(c, carry):
        r0 = pl.multiple_of(c * chunk, chunk)
        rows = pl.ds(r0, chunk)

        def first(h):
            q = q_ref[rows, h * dk:(h + 1) * dk]
            k = k_ref[rows, h * dk:(h + 1) * dk]
            v = v_ref[rows, h * dv:(h + 1) * dv]
            s = state_refs[h][...]
            inner = (_dot_nt(q, k) * dmat_ref[h]).astype(BF16)
            cross = _dot(q, s.astype(BF16)) * dq_ref[h]
            kw = (k.astype(F32) * dk_ref[h]).astype(BF16)
            state_refs[h][...] = s * gc_ref[h] + _dot_tn(kw, v)
            return inner, cross, v

        def second(h, inner, cross, v):
            o = _dot(inner, v) + cross
            o_ref[rows, h * dv:(h + 1) * dv] = _group_norm(o).astype(BF16)

        pending = first(0)
        for h in range(RET_HEADS):
            nxt = first(h + 1) if h + 1 < RET_HEADS else None
            second(h, *pending)
            pending = nxt
        return carry

    lax.fori_loop(0, q_ref.shape[0] // chunk, one_chunk, 0)

    @pl.when(j == pl.num_programs(1) - 1)
    def _():
        for h in range(RET_HEADS):
            sfin_ref[0, h] = state_refs[h][...]


def _ret_prompt(q, k, v, consts, batch, seq):
    hk = q.shape[1]
    hv = v.shape[1]
    dk, dv = hk // RET_HEADS, hv // RET_HEADS
    rb = RET_ROWS
    per_seq = seq // rb
    dmat, dq, dkk, gc = consts
    row = lambda b, j: (b * per_seq + j, 0)
    return pl.pallas_call(
        functools.partial(_ret_prompt_kernel, dk=dk, dv=dv, chunk=RET_CHUNK),
        grid=(batch, per_seq),
        in_specs=[
            pl.BlockSpec((rb, hk), row),
            pl.BlockSpec((rb, hk), row),
            pl.BlockSpec((rb, hv), row),
            _resident(dmat.shape),
            _resident(dq.shape),
            _resident(dkk.shape),
            _resident(gc.shape),
        ],
        out_specs=[
            pl.BlockSpec((rb, hv), row),
            pl.BlockSpec((1, RET_HEADS, dk, dv), lambda b, j: (b, 0, 0, 0)),
        ],
        out_shape=[
            jax.ShapeDtypeStruct((batch * seq, hv), BF16),
            jax.ShapeDtypeStruct((batch, RET_HEADS, dk, dv), F32),
        ],
        scratch_shapes=[pltpu.VMEM((dk, dv), F32)] * RET_HEADS,
        compiler_params=_params(("parallel", "arbitrary"), 40),
        name="ret_prompt",
    )(q, k, v, dmat, dq, dkk, gc)


def _ret_sample_kernel(q_ref, k_ref, v_ref, s_ref, dmat_ref, dq_ref, dk_ref, gc_ref,
                       o_ref, snew_ref, *, dk, dv, dec_seq, seqs):
    rows = seqs * dec_seq
    row_seq = lax.broadcasted_iota(jnp.int32, (rows, 1), 0) // dec_seq
    for h in range(RET_HEADS):
        q = q_ref[:, h * dk:(h + 1) * dk]
        k = k_ref[:, h * dk:(h + 1) * dk]
        v = v_ref[:, h * dv:(h + 1) * dv]
        inner = _dot_nt(q, k) * dmat_ref[h]
        o = _dot(inner.astype(BF16), v)
        kw = k.astype(F32) * dk_ref[h]
        cross = jnp.zeros((rows, dv), F32)
        for s in range(seqs):
            st = s_ref[s, h]
            mine = row_seq == s
            cross = cross + jnp.where(mine, _dot(q, st.astype(BF16)), 0.0)
            kws = jnp.where(mine, kw, 0.0).astype(BF16)
            snew_ref[s, h] = st * gc_ref[h] + _dot_tn(kws, v)
        o = o + cross * dq_ref[h]
        o_ref[:, h * dv:(h + 1) * dv] = _group_norm(o).astype(BF16)


def _ret_sample(q, k, v, state, consts, n_prompt, dec_batch, dec_seq):
    hk = q.shape[1]
    hv = v.shape[1]
    dk, dv = hk // RET_HEADS, hv // RET_HEADS
    seqs = 16 // dec_seq if dec_seq < 16 else 1
    rows = seqs * dec_seq
    base = n_prompt // rows
    dmat, dq, dkk, gc = consts
    row = lambda i: (base + i, 0)
    return pl.pallas_call(
        functools.partial(_ret_sample_kernel, dk=dk, dv=dv, dec_seq=dec_seq, seqs=seqs),
        grid=(dec_batch // seqs,),
        in_specs=[
            pl.BlockSpec((rows, hk), row),
            pl.BlockSpec((rows, hk), row),
            pl.BlockSpec((rows, hv), row),
            pl.BlockSpec((seqs, RET_HEADS, dk, dv), lambda i: (i, 0, 0, 0)),
            _resident(dmat.shape),
            _resident(dq.shape),
            _resident(dkk.shape),
            _resident(gc.shape),
        ],
        out_specs=[
            pl.BlockSpec((rows, hv), lambda i: (i, 0)),
            pl.BlockSpec((seqs, RET_HEADS, dk, dv), lambda i: (i, 0, 0, 0)),
        ],
        out_shape=[
            jax.ShapeDtypeStruct((dec_batch * dec_seq, hv), BF16),
            jax.ShapeDtypeStruct(state.shape, F32),
        ],
        compiler_params=_params(("parallel",), 40),
        name="ret_sample",
    )(q, k, v, state, dmat, dq, dkk, gc)


def _decay_consts(chunk, rows):
    log_gamma = jnp.log1p(-jnp.exp2(-5.0 - jnp.arange(RET_HEADS, dtype=F32)))
    idx = jnp.arange(rows)
    pos = (idx % chunk).astype(F32)
    same = (idx[:, None] // chunk) == (idx[None, :] // chunk)
    diff = pos[:, None] - pos[None, :]
    lg = log_gamma[:, None, None]
    dmat = jnp.where(same & (diff >= 0), jnp.exp(lg * jnp.maximum(diff, 0.0)), 0.0)
    dq = jnp.exp(log_gamma[:, None] * (pos + 1.0))[..., None]
    dk = jnp.exp(log_gamma[:, None] * (chunk - 1.0 - pos))[..., None]
    gc = jnp.exp(log_gamma * chunk)[:, None, None]
    return dmat.astype(F32), dq.astype(F32), dk.astype(F32), gc.astype(F32)


def _mix_ffn_kernel(xp_ref, xs_ref, op_ref, os_ref, gate_ref, gout_ref, wout_ref, gffn_ref, wgu_ref, wdown_ref,
                    h_ref, act_ref, *, d_ff, chunk, n_prompt_tiles):
    o = _pick(op_ref, os_ref, n_prompt_tiles).astype(F32) * gout_ref[...]
    z = (_silu(gate_ref[...].astype(F32)) * o).astype(BF16)
    h1 = _pick(xp_ref, xs_ref, n_prompt_tiles) + _dot(z, wout_ref[...])
    xf = _rms(h1, gffn_ref[...]).astype(BF16)
    for c in range(d_ff // chunk):
        g = _dot(xf, wgu_ref[:, c * chunk:(c + 1) * chunk])
        u = _dot(xf, wgu_ref[:, d_ff + c * chunk:d_ff + (c + 1) * chunk])
        act_ref[:, c * chunk:(c + 1) * chunk] = (_silu(g) * u).astype(BF16)
    h_ref[...] = h1 + _dot(act_ref[...], wdown_ref[...])


def _mix_ffn(x_p, x_s, onorm_p, onorm_s, gate, g_out, w_out, g_ffn, w_gu, w_down):
    d = x_p.shape[1]
    n = x_p.shape[0] + x_s.shape[0]
    hv = onorm_p.shape[1]
    d_ff = w_down.shape[0]
    tm = TOKEN_TILE
    npt = onorm_p.shape[0] // tm
    return pl.pallas_call(
        functools.partial(_mix_ffn_kernel, d_ff=d_ff, chunk=256, n_prompt_tiles=npt),
        grid=(n // tm,),
        in_specs=_split_specs(tm, d, npt) + _split_specs(tm, hv, npt) + [
            pl.BlockSpec((tm, hv), lambda i: (i, 0)),
            _resident((1, hv)),
            _resident(w_out.shape),
            _resident((1, d)),
            _resident(w_gu.shape),
            _resident(w_down.shape),
        ],
        out_specs=pl.BlockSpec((tm, d), lambda i: (i, 0)),
        out_shape=jax.ShapeDtypeStruct((n, d), F32),
        scratch_shapes=[pltpu.VMEM((tm, d_ff), BF16)],
        compiler_params=_params(("parallel",), 56),
        name="mix_ffn",
    )(x_p, x_s, onorm_p, onorm_s, gate, g_out, w_out, g_ffn, w_gu, w_down)


def _rope_lanes(x, cos, sin_signed, half):
    width = x.shape[-1]
    lane = lax.broadcasted_iota(jnp.int32, x.shape, 1)
    first = (lane % (2 * half)) < half
    rot = jnp.where(first, pltpu.roll(x, width - half, axis=1), pltpu.roll(x, half, axis=1))
    return x * cos + rot * sin_signed


def _unit_rms(x, width):
    ms = jnp.sum(x * x, axis=-1, keepdims=True) * (1.0 / width)
    return x * lax.rsqrt(ms + EPS)


HEAD_PAD = 256
V_PAD = 144


def _mla_proj_kernel(h_ref, gmix_ref, gkv_ref, wdq_ref, gqa_ref, wuq_ref, gqn_ref, gqp_ref,
                     wdkvc_ref, wdkvp_ref, gkva_ref, gkpe_ref, wuk_ref, wuvt_ref, gkn_ref,
                     cos_ref, sin_ref,
                     qcat_ref, kcat_ref, vt_ref, ckv_ref, kpe_ref, *, scale):
    h = h_ref[...]
    r = lax.rsqrt(jnp.mean(h * h, axis=-1, keepdims=True) + EPS)
    xn = ((h * r) * gmix_ref[...]).astype(BF16)
    xk = ((h * r) * gkv_ref[...]).astype(BF16)
    cos = cos_ref[...]
    sin = sin_ref[...]
    half = MLA_ROPE // 2

    ckv = _rms(_dot(xk, wdkvc_ref[...]), gkva_ref[...])
    ckv_ref[...] = ckv
    ap = _dot(xk, wdkvp_ref[...])
    kpe = _rope_lanes(_unit_rms(ap, MLA_ROPE) * gkpe_ref[...], cos, sin, half)
    kpe_ref[...] = kpe[:, :MLA_ROPE]
    kpe_b = kpe.astype(BF16)

    cq = _rms(_dot(xn, wdq_ref[...]), gqa_ref[...]).astype(BF16)
    q = _dot(cq, wuq_ref[...])
    cb = ckv.astype(BF16)
    kn = _dot(cb, wuk_ref[...])
    for hd in range(MLA_HEADS):
        c0 = hd * HEAD_PAD
        qn = _unit_rms(q[:, c0:c0 + MLA_NOPE], MLA_NOPE) * gqn_ref[...]
        qp = _unit_rms(q[:, c0 + MLA_NOPE:c0 + HEAD_PAD], MLA_ROPE) * gqp_ref[...]
        qp = _rope_lanes(qp, cos, sin, half)
        qcat_ref[:, c0:c0 + MLA_NOPE] = (qn * scale).astype(BF16)
        qcat_ref[:, c0 + MLA_NOPE:c0 + HEAD_PAD] = (qp * scale).astype(BF16)
        kh = _unit_rms(kn[:, hd * MLA_NOPE:(hd + 1) * MLA_NOPE], MLA_NOPE) * gkn_ref[...]
        kcat_ref[:, c0:c0 + MLA_NOPE] = kh.astype(BF16)
        kcat_ref[:, c0 + MLA_NOPE:c0 + HEAD_PAD] = kpe_b
    vt = _dot_nt(wuvt_ref[...], cb)
    ones = jnp.ones((V_PAD - MLA_DV, vt.shape[1]), BF16)
    for hd in range(MLA_HEADS):
        vt_ref[hd * V_PAD:hd * V_PAD + MLA_DV, :] = vt[hd * MLA_DV:(hd + 1) * MLA_DV, :].astype(BF16)
        vt_ref[hd * V_PAD + MLA_DV:(hd + 1) * V_PAD, :] = ones


def _mla_proj(h, weights, cos_tab, sin_tab, n_prompt_tiles, tiles_per_seq, scale):
    n, d = h.shape
    tm = TOKEN_TILE
    hc = MLA_HEADS * HEAD_PAD
    hn = MLA_HEADS * V_PAD
    kv_lora = weights[7].shape[1]

    def tab_map(i):
        return (jnp.where(i < n_prompt_tiles, i % tiles_per_seq, tiles_per_seq), 0)

    row = lambda i: (i, 0)
    in_specs = [pl.BlockSpec((tm, d), row)] + [_resident(w.shape) for w in weights]
    in_specs += [pl.BlockSpec((tm, LANES), tab_map), pl.BlockSpec((tm, LANES), tab_map)]
    return pl.pallas_call(
        functools.partial(_mla_proj_kernel, scale=scale),
        grid=(n // tm,),
        in_specs=in_specs,
        out_specs=[
            pl.BlockSpec((tm, hc), row),
            pl.BlockSpec((tm, hc), row),
            pl.BlockSpec((hn, tm), lambda i: (0, i)),
            pl.BlockSpec((tm, kv_lora), row),
            pl.BlockSpec((tm, MLA_ROPE), row),
        ],
        out_shape=[
            jax.ShapeDtypeStruct((n, hc), BF16),
            jax.ShapeDtypeStruct((n, hc), BF16),
            jax.ShapeDtypeStruct((hn, n), BF16),
            jax.ShapeDtypeStruct((n, kv_lora), F32),
            jax.ShapeDtypeStruct((n, MLA_ROPE), F32),
        ],
        compiler_params=_params(("parallel",), 48),
        name="mla_proj",
    )(h, *weights, cos_tab, sin_tab)


def _prompt_attn_kernel(qi_ref, ki_ref, q_ref, k_ref, vt_ref, o_ref, *scratch, tile, qblock):
    m_refs = scratch[:MLA_HEADS]
    acc_refs = scratch[MLA_HEADS:2 * MLA_HEADS]
    s_scr = scratch[2 * MLA_HEADS:2 * MLA_HEADS + 2]
    mx_scr = scratch[2 * MLA_HEADS + 2:]
    qi = qi_ref[pl.program_id(1)]
    ki = ki_ref[pl.program_id(1)]

    @pl.when(ki == 0)
    def _():
        for h in range(MLA_HEADS):
            m_refs[h][...] = jnp.full_like(m_refs[h], -jnp.inf)
            acc_refs[h][...] = jnp.zeros_like(acc_refs[h])

    def key_tile(diagonal):
        if diagonal:
            key = lax.broadcasted_iota(jnp.int32, (tile, tile), 0)
            qry = lax.broadcasted_iota(jnp.int32, (tile, tile), 1)
        def scores(h):
            cols = slice(h * HEAD_PAD, (h + 1) * HEAD_PAD)
            s = _dot_nt(k_ref[:, cols], q_ref[:, cols])
            if diagonal:
                s = jnp.where(key <= qry, s, -jnp.inf)
            s_scr[h % 2][...] = s
            mx_scr[h % 2][...] = jnp.max(s, axis=0, keepdims=True)

        def fold(h):
            m_old = m_refs[h][...]
            m_new = jnp.maximum(m_old, mx_scr[h % 2][...])
            m_refs[h][...] = m_new
            p = jnp.exp2(s_scr[h % 2][...] - m_new).astype(BF16)
            acc_refs[h][...] = jnp.exp2(m_old - m_new) * acc_refs[h][...] + _dot(
                vt_ref[h * V_PAD:(h + 1) * V_PAD, :], p)

        scores(0)
        for h in range(MLA_HEADS):
            if h + 1 < MLA_HEADS:
                scores(h + 1)
            fold(h)

    @pl.when(ki < qi)
    def _():
        key_tile(False)

    @pl.when(ki == qi)
    def _():
        key_tile(True)
        outs = []
        for h in range(MLA_HEADS):
            blk = acc_refs[h][...]
            outs.append(blk[:MLA_DV] / blk[MLA_DV:MLA_DV + 1])
        o_ref[...] = jnp.concatenate(outs, axis=0).T


def _prompt_attn(qcat, kcat, vt, batch, seq):
    t = ATTN_TILE
    per_seq = seq // t
    hc = qcat.shape[1]
    hn = MLA_HEADS * MLA_DV
    hvp = vt.shape[0]
    pairs = [(qi, ki) for qi in range(per_seq) for ki in range(qi + 1)]
    qi_tab = jnp.asarray([p[0] for p in pairs], jnp.int32)
    ki_tab = jnp.asarray([p[1] for p in pairs], jnp.int32)
    qrow = lambda b, s, qt, kt: (b * per_seq + qt[s], 0)
    krow = lambda b, s, qt, kt: (b * per_seq + kt[s], 0)
    vcol = lambda b, s, qt, kt: (0, b * per_seq + kt[s])
    grid_spec = pltpu.PrefetchScalarGridSpec(
        num_scalar_prefetch=2,
        grid=(batch, len(pairs)),
        in_specs=[
            pl.BlockSpec((t, hc), qrow),
            pl.BlockSpec((t, hc), krow),
            pl.BlockSpec((hvp, t), vcol),
        ],
        out_specs=pl.BlockSpec((t, hn), qrow),
        scratch_shapes=([pltpu.VMEM((1, t), F32)] * MLA_HEADS + [pltpu.VMEM((V_PAD, t), F32)] * MLA_HEADS
                        + [pltpu.VMEM((t, t), F32)] * 2 + [pltpu.VMEM((1, t), F32)] * 2),
    )
    return pl.pallas_call(
        functools.partial(_prompt_attn_kernel, tile=t, qblock=ATTN_QBLOCK),
        grid_spec=grid_spec,
        out_shape=jax.ShapeDtypeStruct((batch * seq, hn), F32),
        compiler_params=_params(("parallel", "arbitrary"), 40),
        name="prompt_attn",
    )(qi_tab, ki_tab, qcat, kcat, vt)


def _absorb_q_kernel(qcat_ref, gkn_ref, wuk_ref, qa_ref, qps_ref):
    lora = wuk_ref.shape[0]
    for h in range(MLA_HEADS):
        c0 = h * HEAD_PAD
        qg = (qcat_ref[:, c0:c0 + MLA_NOPE].astype(F32) * gkn_ref[...]).astype(BF16)
        qa_ref[:, h * lora:(h + 1) * lora] = _dot_nt(qg, wuk_ref[:, h * MLA_NOPE:(h + 1) * MLA_NOPE])
        qps_ref[:, h * LANES:(h + 1) * LANES] = qcat_ref[:, c0 + MLA_NOPE:c0 + HEAD_PAD].astype(F32)


def _absorb_q(qcat, gkn, wuk, n_prompt, n_sample):
    hc = qcat.shape[1]
    lora = wuk.shape[0]
    tm = min(256, n_sample)
    base = n_prompt // tm
    return pl.pallas_call(
        _absorb_q_kernel,
        grid=(n_sample // tm,),
        in_specs=[
            pl.BlockSpec((tm, hc), lambda i: (base + i, 0)),
            _resident(gkn.shape),
            _resident(wuk.shape),
        ],
        out_specs=[
            pl.BlockSpec((tm, MLA_HEADS * lora), lambda i: (i, 0)),
            pl.BlockSpec((tm, MLA_HEADS * LANES), lambda i: (i, 0)),
        ],
        out_shape=[
            jax.ShapeDtypeStruct((n_sample, MLA_HEADS * lora), F32),
            jax.ShapeDtypeStruct((n_sample, MLA_HEADS * LANES), F32),
        ],
        compiler_params=_params(("parallel",), 32),
        name="absorb_q",
    )(qcat, gkn, wuk)


def _sample_attn_kernel(pt_ref, qa_ref, qp_ref, cnew_ref, pnew_ref, wukt_ref, wuv_ref, cache_c_ref, cache_pt_ref,
                        o_ref, cbuf, pbuf, sem, m_ref, l_ref, acc_ref, lhs_s, qa_s, qp_s, cnew_s, pnew_s,
                        cbb0, cbb1, s_scr0, s_scr1, *, n_pages, ppc, page, dec_seq, lora, key_block):
    b = pl.program_id(0)
    nb = pl.num_programs(0)
    n_chunks = n_pages // ppc
    hq = MLA_HEADS * dec_seq
    n_proj = MLA_HEADS * MLA_NOPE

    def copies(seq, chunk, slot):
        out = []
        for p in range(ppc):
            pg = pt_ref[seq, chunk * ppc + p]
            out.append(pltpu.make_async_copy(cache_c_ref.at[pg], cbuf.at[slot, pl.ds(p * page, page)], sem.at[0, slot]))
            out.append(pltpu.make_async_copy(cache_pt_ref.at[pg], pbuf.at[slot, :, pl.ds(p * page, page)],
                                             sem.at[1, slot]))
        return out

    @pl.when(b == 0)
    def _():
        for cp in copies(0, 0, 0):
            cp.start()
        lhs_s[0:n_proj, :] = wukt_ref[...]

    for h in range(MLA_HEADS):
        qa_s[h * dec_seq:(h + 1) * dec_seq, :] = qa_ref[:, h * lora:(h + 1) * lora]
        qp_s[h * dec_seq:(h + 1) * dec_seq, :] = qp_ref[:, h * LANES:(h + 1) * LANES]
    lhs_s[n_proj:n_proj + hq, :] = qa_s[...].astype(BF16)
    qp = qp_s[...][:, :MLA_ROPE].astype(BF16)
    m_ref[...] = jnp.full_like(m_ref, -jnp.inf)
    l_ref[...] = jnp.zeros_like(l_ref)
    acc_ref[...] = jnp.zeros_like(acc_ref)

    def latent_scores(cb):
        t = cb.shape[0]
        kb = min(key_block, t)
        blocks = []
        for j in range(t // kb):
            kt = _dot_nt(lhs_s[...], cb[j * kb:(j + 1) * kb, :])
            parts = []
            for h in range(MLA_HEADS):
                kh = kt[h * MLA_NOPE:(h + 1) * MLA_NOPE, :]
                ms = jnp.sum(kh * kh, axis=0, keepdims=True) * (1.0 / MLA_NOPE)
                parts.append(kt[n_proj + h * dec_seq:n_proj + (h + 1) * dec_seq, :] * lax.rsqrt(ms + EPS))
            blocks.append(jnp.concatenate(parts, axis=0))
        return blocks[0] if len(blocks) == 1 else jnp.concatenate(blocks, axis=1)

    cbb = (cbb0, cbb1)
    s_scr = (s_scr0, s_scr1)

    def project(slot, par):
        cb = cbuf[slot].astype(BF16)
        cbb[par][...] = cb
        s_scr[par][...] = latent_scores(cb) + _dot(qp, pbuf[slot].astype(BF16))

    def absorb(s, cb):
        m_old = m_ref[...]
        m_new = jnp.maximum(m_old, jnp.max(s, axis=-1, keepdims=True))
        alpha = jnp.exp2(m_old - m_new)
        p = jnp.exp2(s - m_new)
        l_ref[...] = alpha * l_ref[...] + jnp.sum(p, axis=-1, keepdims=True)
        m_ref[...] = m_new
        acc_ref[...] = alpha * acc_ref[...] + _dot(p.astype(BF16), cb)

    def fetch_next(c, slot):
        if c + 1 < n_chunks:
            for cp in copies(b, c + 1, 1 - slot):
                cp.start()
        else:
            @pl.when(b + 1 < nb)
            def _():
                for cp in copies(b + 1, 0, 1 - slot):
                    cp.start()

        for cp in copies(b, c, slot):
            cp.wait()

    for c in range(n_chunks):
        slot = (b * n_chunks + c) % 2 if n_chunks % 2 else c % 2
        fetch_next(c, slot)
        project(slot, c % 2)
        if c:
            absorb(s_scr[(c - 1) % 2][...], cbb[(c - 1) % 2][...])

    cnew_s[...] = jnp.zeros_like(cnew_s)
    pnew_s[...] = jnp.zeros_like(pnew_s)
    cnew_s[0:dec_seq, :] = cnew_ref[...]
    pnew_s[0:dec_seq, :] = pnew_ref[...]
    cnew = cnew_s[...].astype(BF16)
    row_tok = lax.broadcasted_iota(jnp.int32, (hq, LANES), 0) % dec_seq
    col_tok = lax.broadcasted_iota(jnp.int32, (hq, LANES), 1)
    s_new = latent_scores(cnew) + _dot_nt(qp, pnew_s[...].astype(BF16))
    s_new = jnp.where(col_tok <= row_tok, s_new, -jnp.inf)
    last = (n_chunks - 1) % 2
    absorb(jnp.concatenate([s_scr[last][...], s_new], axis=1), jnp.concatenate([cbb[last][...], cnew], axis=0))

    lat = (acc_ref[...] / l_ref[...]).astype(BF16)
    full = _dot(lat, wuv_ref[...])
    for h in range(MLA_HEADS):
        cols = slice(h * MLA_DV, (h + 1) * MLA_DV)
        o_ref[:, cols] = full[h * dec_seq:(h + 1) * dec_seq, cols]


def _sample_attn(page_table, qa, qp, ckv, kpe, wukt, wuv, cache_c, cache_pt, n_prompt, dec_batch, dec_seq):
    n_pages = page_table.shape[1]
    page = cache_c.shape[1]
    lora = cache_c.shape[2]
    ppc = min(PAGES_PER_CHUNK, n_pages)
    hn = MLA_HEADS * MLA_DV
    hq = MLA_HEADS * dec_seq
    base = n_prompt // dec_seq
    srow = lambda b, pt: (b, 0)
    nrow = lambda b, pt: (base + b, 0)
    grid_spec = pltpu.PrefetchScalarGridSpec(
        num_scalar_prefetch=1,
        grid=(dec_batch,),
        in_specs=[
            pl.BlockSpec((dec_seq, MLA_HEADS * lora), srow),
            pl.BlockSpec((dec_seq, MLA_HEADS * LANES), srow),
            pl.BlockSpec((dec_seq, lora), nrow),
            pl.BlockSpec((dec_seq, MLA_ROPE), nrow),
            pl.BlockSpec(wukt.shape, lambda b, pt: (0, 0), pipeline_mode=pl.Buffered(1)),
            pl.BlockSpec(wuv.shape, lambda b, pt: (0, 0), pipeline_mode=pl.Buffered(1)),
            pl.BlockSpec(memory_space=pl.ANY),
            pl.BlockSpec(memory_space=pl.ANY),
        ],
        out_specs=pl.BlockSpec((dec_seq, hn), srow),
        scratch_shapes=[
            pltpu.VMEM((2, ppc * page, lora), F32),
            pltpu.VMEM((2, MLA_ROPE, ppc * page), F32),
            pltpu.SemaphoreType.DMA((2, 2)),
            pltpu.VMEM((hq, 1), F32),
            pltpu.VMEM((hq, 1), F32),
            pltpu.VMEM((hq, lora), F32),
            pltpu.VMEM((MLA_HEADS * MLA_NOPE + hq, lora), BF16),
            pltpu.VMEM((hq, lora), F32),
            pltpu.VMEM((hq, LANES), F32),
            pltpu.VMEM((LANES, lora), F32),
            pltpu.VMEM((LANES, MLA_ROPE), F32),
            pltpu.VMEM((ppc * page, lora), BF16),
            pltpu.VMEM((ppc * page, lora), BF16),
            pltpu.VMEM((hq, ppc * page), F32),
            pltpu.VMEM((hq, ppc * page), F32),
        ],
    )
    return pl.pallas_call(
        functools.partial(_sample_attn_kernel, n_pages=n_pages, ppc=ppc, page=page, dec_seq=dec_seq, lora=lora,
                          key_block=SAMPLE_KEY_BLOCK),
        grid_spec=grid_spec,
        out_shape=jax.ShapeDtypeStruct((dec_batch * dec_seq, hn), F32),
        compiler_params=_params(("arbitrary",), 40),
        name="sample_attn",
    )(page_table, qa, qp, ckv, kpe, wukt, wuv, cache_c, cache_pt)


def _route_kernel(h_ref, op_ref, os_ref, wo_ref, gffn_ref, wr_ref, tri_ref, h3_ref, xf_ref, info_ref, infot_ref,
                  cnt_ref, carry_ref, *, n_prompt_tiles):
    i = pl.program_id(0)

    @pl.when(i == 0)
    def _():
        carry_ref[...] = jnp.zeros_like(carry_ref)

    o = jnp.where(i < n_prompt_tiles, op_ref[...], os_ref[...])
    h3 = h_ref[...] + _dot(o.astype(BF16), wo_ref[...])
    h3_ref[...] = h3
    xf = _rms(h3, gffn_ref[...])
    xf_ref[...] = xf
    logits = _dot(xf.astype(BF16), wr_ref[...])
    lane = lax.broadcasted_iota(jnp.int32, logits.shape, 1).astype(F32)
    lg = jnp.where(lane < N_EXPERTS, logits, -jnp.inf)
    v1 = jnp.max(lg, axis=-1, keepdims=True)
    i1 = jnp.min(jnp.where(lg == v1, lane, float(LANES)), axis=-1, keepdims=True)
    lg2 = jnp.where(lane == i1, -jnp.inf, lg)
    v2 = jnp.max(lg2, axis=-1, keepdims=True)
    i2 = jnp.min(jnp.where(lg2 == v2, lane, float(LANES)), axis=-1, keepdims=True)
    e = jnp.exp(v2 - v1)
    g1 = 1.0 / (1.0 + e)
    g2 = e / (1.0 + e)
    oh1 = lane == i1
    oh2 = lane == i2
    chosen = jnp.where(oh1 | oh2, 1.0, 0.0)
    cum = _dot(tri_ref[...], chosen.astype(BF16))
    before = cum - chosen + carry_ref[...]
    pos1 = jnp.sum(jnp.where(oh1, before, 0.0), axis=-1, keepdims=True)
    pos2 = jnp.sum(jnp.where(oh2, before, 0.0), axis=-1, keepdims=True)
    carry_ref[...] = carry_ref[...] + cum[cum.shape[0] - 1:, :]
    cnt_ref[...] = carry_ref[...]
    info = jnp.where(lane == 0, i1, 0.0)
    info = jnp.where(lane == 1, i2, info)
    info = jnp.where(lane == 2, pos1, info)
    info = jnp.where(lane == 3, pos2, info)
    info = jnp.where(lane == 4, g1, info)
    info = jnp.where(lane == 5, g2, info)
    info_ref[...] = info
    infot_ref[...] = info.T[:8, :]


def _route(h2, o_attn_p, o_attn_s, w_o, g_ffn, w_r, tri):
    n, d = h2.shape
    tm = TOKEN_TILE
    npt = o_attn_p.shape[0] // tm
    hn = o_attn_p.shape[1]
    row = lambda i: (i, 0)
    return pl.pallas_call(
        functools.partial(_route_kernel, n_prompt_tiles=npt),
        grid=(n // tm,),
        in_specs=[
            pl.BlockSpec((tm, d), row),
            pl.BlockSpec((tm, hn), lambda i: (jnp.minimum(i, npt - 1), 0)),
            pl.BlockSpec((tm, hn), lambda i: (jnp.maximum(i - npt, 0), 0)),
            _resident(w_o.shape),
            _resident((1, d)),
            _resident(w_r.shape),
            _resident(tri.shape),
        ],
        out_specs=[
            pl.BlockSpec((tm, d), row),
            pl.BlockSpec((tm, d), row),
            pl.BlockSpec((tm, LANES), row),
            pl.BlockSpec((8, tm), lambda i: (0, i)),
            pl.BlockSpec((1, LANES), lambda i: (0, 0)),
        ],
        out_shape=[
            jax.ShapeDtypeStruct((n, d), F32),
            jax.ShapeDtypeStruct((n, d), F32),
            jax.ShapeDtypeStruct((n, LANES), F32),
            jax.ShapeDtypeStruct((8, n), F32),
            jax.ShapeDtypeStruct((1, LANES), F32),
        ],
        scratch_shapes=[pltpu.VMEM((1, LANES), F32)],
        compiler_params=_params(("arbitrary",), 40),
        name="route",
    )(h2, o_attn_p, o_attn_s, w_o, g_ffn, w_r, tri)


def _scatter_kernel(dest_ref, x_ref, xs_in_ref, xs_ref, sem, *, tm):
    del xs_in_ref

    def issue(r, carry):
        for k in range(2):
            d = dest_ref[0, 0, k * tm + r]
            pltpu.make_async_copy(x_ref.at[pl.ds(r, 1)], xs_ref.at[pl.ds(d, 1)], sem).start(priority=k)
        return carry

    lax.fori_loop(0, tm, issue, 0, unroll=8)

    def drain(r, carry):
        for k in range(2):
            pltpu.make_async_copy(x_ref.at[pl.ds(r, 1)], xs_ref.at[pl.ds(0, 1)], sem).wait()
        return carry

    lax.fori_loop(0, tm, drain, 0, unroll=8)


def _scatter_rows(dest, xf, xs0):
    n, d = xf.shape
    tm = TOKEN_TILE
    return pl.pallas_call(
        functools.partial(_scatter_kernel, tm=tm),
        grid=(n // tm,),
        in_specs=[
            pl.BlockSpec((1, 1, 2 * tm), lambda i: (i, 0, 0), memory_space=pltpu.SMEM),
            pl.BlockSpec((tm, d), lambda i: (i, 0)),
            pl.BlockSpec(memory_space=pl.ANY),
        ],
        out_specs=pl.BlockSpec(memory_space=pl.ANY),
        out_shape=jax.ShapeDtypeStruct(xs0.shape, xs0.dtype),
        scratch_shapes=[pltpu.SemaphoreType.DMA(())],
        input_output_aliases={2: 0},
        compiler_params=_params(("arbitrary",), 32),
        name="scatter_rows",
    )(dest, xf, xs0)


def _experts_kernel(te_ref, tr_ref, x_ref, wg_ref, wu_ref, wd_ref, o_ref, acc_ref, xb_ref, *, sub):
    t = pl.program_id(0)
    c = pl.program_id(1)
    rows = tr_ref[t]
    n_sub = x_ref.shape[0] // sub

    @pl.when(c == 0)
    def _():
        xb_ref[...] = x_ref[...].astype(BF16)
        acc_ref[...] = jnp.zeros_like(acc_ref)

    for sb in range(n_sub):
        r = slice(sb * sub, (sb + 1) * sub)

        @pl.when(sb * sub < rows)
        def _():
            xb = xb_ref[r, :]
            act = (_silu(_dot(xb, wg_ref[0])) * _dot(xb, wu_ref[0])).astype(BF16)
            acc_ref[r, :] += _dot(act, wd_ref[0])

    @pl.when(c == pl.num_programs(1) - 1)
    def _():
        o_ref[...] = acc_ref[...]


def _experts(tile_expert, tile_rows, xs, w_gu, w_down):
    rows, d = xs.shape
    tm = EXPERT_TILE
    fc = FF_CHUNK
    d_ff = w_down.shape[1]
    n_c = d_ff // fc
    n_tiles = rows // tm

    def ceff(t, c, tr):
        return jnp.where(tr[t] > 0, c, n_c - 1)

    grid_spec = pltpu.PrefetchScalarGridSpec(
        num_scalar_prefetch=2,
        grid=(n_tiles, n_c),
        in_specs=[
            pl.BlockSpec((tm, d), lambda t, c, te, tv: (t, 0)),
            pl.BlockSpec((1, d, fc), lambda t, c, te, tv: (te[t], 0, ceff(t, c, tv))),
            pl.BlockSpec((1, d, fc), lambda t, c, te, tv: (te[t], 0, n_c + ceff(t, c, tv))),
            pl.BlockSpec((1, fc, d), lambda t, c, te, tv: (te[t], ceff(t, c, tv), 0)),
        ],
        out_specs=pl.BlockSpec((tm, d), lambda t, c, te, tv: (t, 0)),
        scratch_shapes=[pltpu.VMEM((tm, d), F32), pltpu.VMEM((tm, d), BF16)],
    )
    return pl.pallas_call(
        functools.partial(_experts_kernel, sub=EXPERT_SUB),
        grid_spec=grid_spec,
        out_shape=jax.ShapeDtypeStruct((rows, d), F32),
        compiler_params=_params(("arbitrary", "arbitrary"), 52),
        name="experts",
    )(tile_expert, tile_rows, xs, w_gu, w_gu, w_down)


def _combine_kernel(dest_ref, h_ref, info_ref, ys_ref, yp_ref, ysm_ref, buf, sem, *, tm, n_prompt_tiles):
    def issue(r, carry):
        for k in range(2):
            d = dest_ref[0, 0, k * tm + r]
            pltpu.make_async_copy(ys_ref.at[pl.ds(d, 1)], buf.at[k, pl.ds(r, 1)], sem).start(priority=k)
        return carry

    lax.fori_loop(0, tm, issue, 0, unroll=8)

    def drain(r, carry):
        for k in range(2):
            pltpu.make_async_copy(ys_ref.at[pl.ds(0, 1)], buf.at[k, pl.ds(r, 1)], sem).wait()
        return carry

    lax.fori_loop(0, tm, drain, 0, unroll=8)
    info = info_ref[...]
    y = h_ref[...] + info[:, 4:5] * buf[0] + info[:, 5:6] * buf[1]

    @pl.when(pl.program_id(0) < n_prompt_tiles)
    def _():
        yp_ref[...] = y

    @pl.when(pl.program_id(0) >= n_prompt_tiles)
    def _():
        ysm_ref[...] = y


def _combine(dest, h3, info, ys, n_prompt):
    n, d = h3.shape
    tm = TOKEN_TILE
    npt = n_prompt // tm
    return pl.pallas_call(
        functools.partial(_combine_kernel, tm=tm, n_prompt_tiles=npt),
        grid=(n // tm,),
        in_specs=[
            pl.BlockSpec((1, 1, 2 * tm), lambda i: (i, 0, 0), memory_space=pltpu.SMEM),
            pl.BlockSpec((tm, d), lambda i: (i, 0)),
            pl.BlockSpec((tm, LANES), lambda i: (i, 0)),
            pl.BlockSpec(memory_space=pl.ANY),
        ],
        out_specs=[
            pl.BlockSpec((tm, d), lambda i: (jnp.minimum(i, npt - 1), 0)),
            pl.BlockSpec((tm, d), lambda i: (jnp.maximum(i - npt, 0), 0)),
        ],
        out_shape=[
            jax.ShapeDtypeStruct((n_prompt, d), F32),
            jax.ShapeDtypeStruct((n - n_prompt, d), F32),
        ],
        scratch_shapes=[pltpu.VMEM((2, tm, d), F32), pltpu.SemaphoreType.DMA(())],
        compiler_params=_params(("arbitrary",), 32),
        name="combine",
    )(dest, h3, info, ys)


def _rope_tables(pos, dim):
    inv = ROPE_BASE ** (-jnp.arange(0, dim, 2, dtype=F32) / dim)
    ang = pos.astype(F32)[:, None] * inv[None, :]
    return jnp.cos(ang), jnp.sin(ang)


def kernel(x_prompt, x_sample, state_ret, cache_ckv, cache_kpe, page_table, g_pre_mix, g_pre_ffn, ret_w_in, ret_g_out, ret_w_out, mla_g_kv_in, mla_w_dkv, mla_g_kv_a, mla_g_kpe, mla_w_uk, mla_w_uv, mla_g_kn, mla_w_dq, mla_g_qa, mla_w_uq, mla_g_qn, mla_g_qp, mla_w_o, ffn_w_gu, ffn_w_down, moe_w_router, moe_w_gu, moe_w_down):
    batch, seq, d = x_prompt.shape
    dec_batch, dec_seq, _ = x_sample.shape
    n_prompt = batch * seq
    n_sample = dec_batch * dec_seq
    n = n_prompt + n_sample
    tm = TOKEN_TILE
    past = page_table.shape[1] * cache_ckv.shape[1]
    hv = ret_g_out.shape[1]
    hk = (ret_w_in.shape[2] - 2 * hv) // 2
    kv_lora = mla_g_kv_a.shape[0]
    assert seq % tm == 0 and n_sample % tm == 0 and seq % ATTN_TILE == 0 and seq % RET_ROWS == 0
    n_prompt_tiles = n_prompt // tm
    tiles_per_seq = seq // tm

    x_p = x_prompt.reshape(n_prompt, d)
    x_s = x_sample.reshape(n_sample, d)

    pos = jnp.concatenate([jnp.arange(seq), past + (jnp.arange(tm) % dec_seq)])
    cos_r, sin_r = _rope_tables(pos, hk // RET_HEADS)
    cos_m, sin_m = _rope_tables(pos, MLA_ROPE)
    cos_m = jnp.tile(cos_m, (1, 2 * LANES // MLA_ROPE))
    sin_m = jnp.tile(jnp.concatenate([-sin_m, sin_m], axis=-1), (1, LANES // MLA_ROPE))

    q, k, v, gate = _ret_in(x_p, x_s, g_pre_mix[0][None], ret_w_in[0].astype(BF16), cos_r, sin_r,
                            n_prompt_tiles, tiles_per_seq, hk, hv)
    onorm_p, s_prompt = _ret_prompt(q, k, v, _decay_consts(RET_CHUNK, RET_CHUNK), batch, seq)
    seqs = 16 // dec_seq if dec_seq < 16 else 1
    onorm_s, s_sample = _ret_sample(q, k, v, state_ret[0], _decay_consts(dec_seq, seqs * dec_seq),
                                    n_prompt, dec_batch, dec_seq)
    h2 = _mix_ffn(x_p, x_s, onorm_p, onorm_s, gate, ret_g_out[0][None], ret_w_out[0].astype(BF16),
                  g_pre_ffn[0][None], ffn_w_gu[0].astype(BF16), ffn_w_down[0].astype(BF16))

    q_lora = mla_w_dq.shape[2]
    head_dim = MLA_NOPE + MLA_ROPE
    w_uq = jnp.pad(mla_w_uq[0].reshape(q_lora, MLA_HEADS, head_dim), ((0, 0), (0, 0), (0, HEAD_PAD - head_dim)))
    w_uq = w_uq.reshape(q_lora, MLA_HEADS * HEAD_PAD).astype(BF16)
    w_dkv_c = mla_w_dkv[:, :kv_lora].astype(BF16)
    w_dkv_p = jnp.pad(mla_w_dkv[:, kv_lora:], ((0, 0), (0, LANES - MLA_ROPE))).astype(BF16)
    pad_rope = lambda g: jnp.pad(g, (0, LANES - MLA_ROPE))[None]
    weights = [
        g_pre_mix[1][None], mla_g_kv_in[None], mla_w_dq[0].astype(BF16), mla_g_qa[0][None], w_uq,
        mla_g_qn[0][None], pad_rope(mla_g_qp[0]),
        w_dkv_c, w_dkv_p, mla_g_kv_a[None], pad_rope(mla_g_kpe),
        mla_w_uk.astype(BF16), mla_w_uv.T.astype(BF16), mla_g_kn[None],
    ]
    scale = float(head_dim ** -0.5 * 1.4426950408889634)
    qcat, kcat, vt, ckv, kpe = _mla_proj(h2, weights, cos_m, sin_m, n_prompt_tiles, tiles_per_seq, scale)
    o_attn_p = _prompt_attn(qcat, kcat, vt, batch, seq)
    qa, qps = _absorb_q(qcat, mla_g_kn[None], mla_w_uk.astype(BF16), n_prompt, n_sample)
    o_attn_s = _sample_attn(page_table, qa, qps, ckv, kpe, mla_w_uk.T.astype(BF16), mla_w_uv.astype(BF16),
                            cache_ckv, cache_kpe.transpose(0, 2, 1), n_prompt, dec_batch, dec_seq)

    w_r = jnp.pad(moe_w_router[0], ((0, 0), (0, LANES - N_EXPERTS))).astype(BF16)
    tri = (jnp.arange(tm)[:, None] >= jnp.arange(tm)[None, :]).astype(BF16)
    h3, xf, info, info_t, counts = _route(h2, o_attn_p, o_attn_s, mla_w_o[0].astype(BF16), g_pre_ffn[1][None],
                                          w_r, tri)

    te = EXPERT_TILE
    counts = counts[0, :N_EXPERTS].astype(jnp.int32)
    tiles_e = (counts + te - 1) // te
    tile_end = jnp.cumsum(tiles_e)
    offsets = (tile_end - tiles_e) * te
    n_tiles = (2 * n) // te + N_EXPERTS
    tile_ids = jnp.arange(n_tiles, dtype=jnp.int32)
    tile_valid = tile_ids < tile_end[-1]
    tile_expert = jnp.minimum(jnp.sum(tile_ids[:, None] >= tile_end[None, :], axis=1), N_EXPERTS - 1)
    first_tile = jnp.zeros_like(tile_ids)
    count_of = jnp.zeros_like(tile_ids)
    for e in range(N_EXPERTS):
        first_tile = jnp.where(tile_expert == e, tile_end[e] - tiles_e[e], first_tile)
        count_of = jnp.where(tile_expert == e, counts[e], count_of)
    tile_rows = jnp.where(tile_valid, jnp.clip(count_of - (tile_ids - first_tile) * te, 0, te), 0).astype(jnp.int32)
    last_expert = jnp.sum((tile_end[-1] - 1) >= tile_end).astype(jnp.int32)
    tile_expert = jnp.where(tile_valid, tile_expert, last_expert).astype(jnp.int32)
    idx = info_t[0:2].astype(jnp.int32)
    base = jnp.zeros_like(idx)
    for e in range(N_EXPERTS):
        base = jnp.where(idx == e, offsets[e], base)
    dest = base + info_t[2:4].astype(jnp.int32)
    dest = dest.reshape(2, n // tm, tm).transpose(1, 0, 2).reshape(n // tm, 1, 2 * tm)

    xs = _scatter_rows(dest, xf, jnp.zeros((n_tiles * te, d), F32))
    ys = _experts(tile_expert, tile_rows, xs, moe_w_gu[0].astype(BF16), moe_w_down[0].astype(BF16))
    y_p, y_s = _combine(dest, h3, info, ys, n_prompt)

    return (
        y_p.reshape(batch, seq, d),
        y_s.reshape(dec_batch, dec_seq, d),
        s_prompt[None],
        s_sample[None],
        ckv[:n_prompt].reshape(batch, seq, kv_lora),
        kpe[:n_prompt].reshape(batch, seq, MLA_ROPE),
        ckv[n_prompt:].reshape(dec_batch, dec_seq, kv_lora),
        kpe[n_prompt:].reshape(dec_batch, dec_seq, MLA_ROPE),
    )
```

```python
import functools

import jax
import jax.numpy as jnp
from jax import lax
from jax.experimental import pallas as pl
from jax.experimental.pallas import tpu as pltpu

F32 = jnp.float32
BF16 = jnp.bfloat16
EPS = 1e-6
ROPE_BASE = 10000.0

RET_HEADS = 4
RET_CHUNK = 128
MLA_HEADS = 8
MLA_NOPE = 128
MLA_ROPE = 64
MLA_DV = 128
N_EXPERTS = 8
LANES = 128

TOKEN_TILE = 512
RET_ROWS = 512
ATTN_TILE = 512
ATTN_QBLOCK = 256
EXPERT_TILE = 1024
EXPERT_SUB = 1024
FF_CHUNK = 512
PAGES_PER_CHUNK = 32
SAMPLE_KEY_BLOCK = 256
MIB = 1024 * 1024


def _params(semantics, vmem_mib, flags=None):
    return pltpu.CompilerParams(dimension_semantics=semantics, vmem_limit_bytes=vmem_mib * MIB, flags=flags)


def _resident(shape):
    return pl.BlockSpec(shape, lambda *_: (0,) * len(shape), pipeline_mode=pl.Buffered(1))


def _rms(x, g):
    ms = jnp.mean(x * x, axis=-1, keepdims=True)
    return (x * lax.rsqrt(ms + EPS)) * g


def _dot(a, b):
    return jnp.dot(a, b, preferred_element_type=F32)


def _dot_nt(a, b):
    return lax.dot_general(a, b, (((1,), (1,)), ((), ())), preferred_element_type=F32)


def _dot_tn(a, b):
    return lax.dot_general(a, b, (((0,), (0,)), ((), ())), preferred_element_type=F32)


def _silu(x):
    return x * jax.nn.sigmoid(x)


def _split_specs(tm, width, n_prompt_tiles):
    return [
        pl.BlockSpec((tm, width), lambda i: (jnp.minimum(i, n_prompt_tiles - 1), 0)),
        pl.BlockSpec((tm, width), lambda i: (jnp.maximum(i - n_prompt_tiles, 0), 0), pipeline_mode=pl.Buffered(1)),
    ]


def _pick(p_ref, s_ref, n_prompt_tiles):
    return jnp.where(pl.program_id(0) < n_prompt_tiles, p_ref[...], s_ref[...])


def _ret_in_kernel(xp_ref, xs_ref, g_ref, w_ref, cos_ref, sin_ref, q_ref, k_ref, v_ref, gate_ref,
                   *, hk, hv, dk, n_prompt_tiles):
    xn = _rms(_pick(xp_ref, xs_ref, n_prompt_tiles), g_ref[...]).astype(BF16)
    cos = cos_ref[...]
    sin = sin_ref[...]
    half = dk // 2

    def rope_store(dst, col, acc, scale):
        x1 = acc[:, :half]
        x2 = acc[:, half:]
        dst[:, col:col + half] = ((x1 * cos - x2 * sin) * scale).astype(BF16)
        dst[:, col + half:col + dk] = ((x1 * sin + x2 * cos) * scale).astype(BF16)

    for h in range(hk // dk):
        rope_store(q_ref, h * dk, _dot(xn, w_ref[:, h * dk:(h + 1) * dk]), 1.0)
    for h in range(hk // dk):
        rope_store(k_ref, h * dk, _dot(xn, w_ref[:, hk + h * dk:hk + (h + 1) * dk]), dk ** -0.5)
    step = 512
    for c in range(hv // step):
        v_ref[:, c * step:(c + 1) * step] = _dot(xn, w_ref[:, 2 * hk + c * step:2 * hk + (c + 1) * step]).astype(BF16)
    for c in range(hv // step):
        gate_ref[:, c * step:(c + 1) * step] = _dot(
            xn, w_ref[:, 2 * hk + hv + c * step:2 * hk + hv + (c + 1) * step]).astype(BF16)


def _ret_in(x_p, x_s, g, w, cos_tab, sin_tab, n_prompt_tiles, tiles_per_seq, hk, hv):
    d = x_p.shape[1]
    n = x_p.shape[0] + x_s.shape[0]
    tm = TOKEN_TILE
    dk = hk // RET_HEADS

    def tab_map(i):
        return (jnp.where(i < n_prompt_tiles, i % tiles_per_seq, tiles_per_seq), 0)

    return pl.pallas_call(
        functools.partial(_ret_in_kernel, hk=hk, hv=hv, dk=dk, n_prompt_tiles=n_prompt_tiles),
        grid=(n // tm,),
        in_specs=_split_specs(tm, d, n_prompt_tiles) + [
            _resident((1, d)),
            _resident(w.shape),
            pl.BlockSpec((tm, dk // 2), tab_map),
            pl.BlockSpec((tm, dk // 2), tab_map),
        ],
        out_specs=[
            pl.BlockSpec((tm, hk), lambda i: (i, 0)),
            pl.BlockSpec((tm, hk), lambda i: (i, 0)),
            pl.BlockSpec((tm, hv), lambda i: (i, 0)),
            pl.BlockSpec((tm, hv), lambda i: (i, 0)),
        ],
        out_shape=[
            jax.ShapeDtypeStruct((n, hk), BF16),
            jax.ShapeDtypeStruct((n, hk), BF16),
            jax.ShapeDtypeStruct((n, hv), BF16),
            jax.ShapeDtypeStruct((n, hv), BF16),
        ],
        compiler_params=_params(("parallel",), 48),
        name="ret_in",
    )(x_p, x_s, g, w, cos_tab, sin_tab)


def _group_norm(o):
    mu = jnp.mean(o, axis=-1, keepdims=True)
    var = jnp.mean(jnp.square(o - mu), axis=-1, keepdims=True)
    return (o - mu) * lax.rsqrt(var + EPS)


def _ret_prompt_kernel(q_ref, k_ref, v_ref, dmat_ref, dq_ref, dk_ref, gc_ref, o_ref, sfin_ref, *state_refs,
                       dk, dv, chunk):
    j = pl.program_id(1)

    @pl.when(j == 0)
    def _():
        for h in range(RET_HEADS):
            state_refs[h][...] = jnp.zeros_like(state_refs[h])

    def one_chunk(c, carry):
        r0 = pl.multiple_of(c * chunk, chunk)
        rows = pl.ds(r0, chunk)

        def first(h):
            q = q_ref[rows, h * dk:(h + 1) * dk]
            k = k_ref[rows, h * dk:(h + 1) * dk]
            v = v_ref[rows, h * dv:(h + 1) * dv]
            s = state_refs[h][...]
            inner = (_dot_nt(q, k) * dmat_ref[h]).astype(BF16)
            cross = _dot(q, s.astype(BF16)) * dq_ref[h]
            kw = (k.astype(F32) * dk_ref[h]).astype(BF16)
            state_refs[h][...] = s * gc_ref[h] + _dot_tn(kw, v)
            return inner, cross, v

        def second(h, inner, cross, v):
            o = _dot(inner, v) + cross
            o_ref[rows, h * dv:(h + 1) * dv] = _group_norm(o).astype(BF16)

        pending = first(0)
        for h in range(RET_HEADS):
            nxt = first(h + 1) if h + 1 < RET_HEADS else None
            second(h, *pending)
            pending = nxt
        return carry

    lax.fori_loop(0, q_ref.shape[0] // chunk, one_chunk, 0)

    @pl.when(j == pl.num_programs(1) - 1)
    def _():
        for h in range(RET_HEADS):
            sfin_ref[0, h] = state_refs[h][...]


def _ret_prompt(q, k, v, consts, batch, seq):
    hk = q.shape[1]
    hv = v.shape[1]
    dk, dv = hk // RET_HEADS, hv // RET_HEADS
    rb = RET_ROWS
    per_seq = seq // rb
    dmat, dq, dkk, gc = consts
    row = lambda b, j: (b * per_seq + j, 0)
    return pl.pallas_call(
        functools.partial(_ret_prompt_kernel, dk=dk, dv=dv, chunk=RET_CHUNK),
        grid=(batch, per_seq),
        in_specs=[
            pl.BlockSpec((rb, hk), row),
            pl.BlockSpec((rb, hk), row),
            pl.BlockSpec((rb, hv), row),
            _resident(dmat.shape),
            _resident(dq.shape),
            _resident(dkk.shape),
            _resident(gc.shape),
        ],
        out_specs=[
            pl.BlockSpec((rb, hv), row),
            pl.BlockSpec((1, RET_HEADS, dk, dv), lambda b, j: (b, 0, 0, 0)),
        ],
        out_shape=[
            jax.ShapeDtypeStruct((batch * seq, hv), BF16),
            jax.ShapeDtypeStruct((batch, RET_HEADS, dk, dv), F32),
        ],
        scratch_shapes=[pltpu.VMEM((dk, dv), F32)] * RET_HEADS,
        compiler_params=_params(("parallel", "arbitrary"), 40),
        name="ret_prompt",
    )(q, k, v, dmat, dq, dkk, gc)


def _ret_sample_kernel(q_ref, k_ref, v_ref, s_ref, dmat_ref, dq_ref, dk_ref, gc_ref,
                       o_ref, snew_ref, *, dk, dv, dec_seq, seqs):
    rows = seqs * dec_seq
    row_seq = lax.broadcasted_iota(jnp.int32, (rows, 1), 0) // dec_seq
    for h in range(RET_HEADS):
        q = q_ref[:, h * dk:(h + 1) * dk]
        k = k_ref[:, h * dk:(h + 1) * dk]
        v = v_ref[:, h * dv:(h + 1) * dv]
        inner = _dot_nt(q, k) * dmat_ref[h]
        o = _dot(inner.astype(BF16), v)
        kw = k.astype(F32) * dk_ref[h]
        cross = jnp.zeros((rows, dv), F32)
        for s in range(seqs):
            st = s_ref[s, h]
            mine = row_seq == s
            cross = cross + jnp.where(mine, _dot(q, st.astype(BF16)), 0.0)
            kws = jnp.where(mine, kw, 0.0).astype(BF16)
            snew_ref[s, h] = st * gc_ref[h] + _dot_tn(kws, v)
        o = o + cross * dq_ref[h]
        o_ref[:, h * dv:(h + 1) * dv] = _group_norm(o).astype(BF16)


def _ret_sample(q, k, v, state, consts, n_prompt, dec_batch, dec_seq):
    hk = q.shape[1]
    hv = v.shape[1]
    dk, dv = hk // RET_HEADS, hv // RET_HEADS
    seqs = 16 // dec_seq if dec_seq < 16 else 1
    rows = seqs * dec_seq
    base = n_prompt // rows
    dmat, dq, dkk, gc = consts
    row = lambda i: (base + i, 0)
    return pl.pallas_call(
        functools.partial(_ret_sample_kernel, dk=dk, dv=dv, dec_seq=dec_seq, seqs=seqs),
        grid=(dec_batch // seqs,),
        in_specs=[
            pl.BlockSpec((rows, hk), row),
            pl.BlockSpec((rows, hk), row),
            pl.BlockSpec((rows, hv), row),
            pl.BlockSpec((seqs, RET_HEADS, dk, dv), lambda i: (i, 0, 0, 0)),
            _resident(dmat.shape),
            _resident(dq.shape),
            _resident(dkk.shape),
            _resident(gc.shape),
        ],
        out_specs=[
            pl.BlockSpec((rows, hv), lambda i: (i, 0)),
            pl.BlockSpec((seqs, RET_HEADS, dk, dv), lambda i: (i, 0, 0, 0)),
        ],
        out_shape=[
            jax.ShapeDtypeStruct((dec_batch * dec_seq, hv), BF16),
            jax.ShapeDtypeStruct(state.shape, F32),
        ],
        compiler_params=_params(("parallel",), 40),
        name="ret_sample",
    )(q, k, v, state, dmat, dq, dkk, gc)


def _decay_consts(chunk, rows):
    log_gamma = jnp.log1p(-jnp.exp2(-5.0 - jnp.arange(RET_HEADS, dtype=F32)))
    idx = jnp.arange(rows)
    pos = (idx % chunk).astype(F32)
    same = (idx[:, None] // chunk) == (idx[None, :] // chunk)
    diff = pos[:, None] - pos[None, :]
    lg = log_gamma[:, None, None]
    dmat = jnp.where(same & (diff >= 0), jnp.exp(lg * jnp.maximum(diff, 0.0)), 0.0)
    dq = jnp.exp(log_gamma[:, None] * (pos + 1.0))[..., None]
    dk = jnp.exp(log_gamma[:, None] * (chunk - 1.0 - pos))[..., None]
    gc = jnp.exp(log_gamma * chunk)[:, None, None]
    return dmat.astype(F32), dq.astype(F32), dk.astype(F32), gc.astype(F32)


def _mix_ffn_kernel(xp_ref, xs_ref, op_ref, os_ref, gate_ref, gout_ref, wout_ref, gffn_ref, wgu_ref, wdown_ref,
                    h_ref, act_ref, *, d_ff, chunk, n_prompt_tiles):
    o = _pick(op_ref, os_ref, n_prompt_tiles).astype(F32) * gout_ref[...]
    z = (_silu(gate_ref[...].astype(F32)) * o).astype(BF16)
    h1 = _pick(xp_ref, xs_ref, n_prompt_tiles) + _dot(z, wout_ref[...])
    xf = _rms(h1, gffn_ref[...]).astype(BF16)
    for c in range(d_ff // chunk):
        g = _dot(xf, wgu_ref[:, c * chunk:(c + 1) * chunk])
        u = _dot(xf, wgu_ref[:, d_ff + c * chunk:d_ff + (c + 1) * chunk])
        act_ref[:, c * chunk:(c + 1) * chunk] = (_silu(g) * u).astype(BF16)
    h_ref[...] = h1 + _dot(act_ref[...], wdown_ref[...])


def _mix_ffn(x_p, x_s, onorm_p, onorm_s, gate, g_out, w_out, g_ffn, w_gu, w_down):
    d = x_p.shape[1]
    n = x_p.shape[0] + x_s.shape[0]
    hv = onorm_p.shape[1]
    d_ff = w_down.shape[0]
    tm = TOKEN_TILE
    npt = onorm_p.shape[0] // tm
    return pl.pallas_call(
        functools.partial(_mix_ffn_kernel, d_ff=d_ff, chunk=256, n_prompt_tiles=npt),
        grid=(n // tm,),
        in_specs=_split_specs(tm, d, npt) + _split_specs(tm, hv, npt) + [
            pl.BlockSpec((tm, hv), lambda i: (i, 0)),
            _resident((1, hv)),
            _resident(w_out.shape),
            _resident((1, d)),
            _resident(w_gu.shape),
            _resident(w_down.shape),
        ],
        out_specs=pl.BlockSpec((tm, d), lambda i: (i, 0)),
        out_shape=jax.ShapeDtypeStruct((n, d), F32),
        scratch_shapes=[pltpu.VMEM((tm, d_ff), BF16)],
        compiler_params=_params(("parallel",), 56),
        name="mix_ffn",
    )(x_p, x_s, onorm_p, onorm_s, gate, g_out, w_out, g_ffn, w_gu, w_down)


def _rope_lanes(x, cos, sin_signed, half):
    width = x.shape[-1]
    lane = lax.broadcasted_iota(jnp.int32, x.shape, 1)
    first = (lane % (2 * half)) < half
    rot = jnp.where(first, pltpu.roll(x, width - half, axis=1), pltpu.roll(x, half, axis=1))
    return x * cos + rot * sin_signed


def _unit_rms(x, width):
    ms = jnp.sum(x * x, axis=-1, keepdims=True) * (1.0 / width)
    return x * lax.rsqrt(ms + EPS)


HEAD_PAD = 256
V_PAD = 144


def _mla_proj_kernel(h_ref, gmix_ref, gkv_ref, wdq_ref, gqa_ref, wuq_ref, gqn_ref, gqp_ref,
                     wdkvc_ref, wdkvp_ref, gkva_ref, gkpe_ref, wuk_ref, wuvt_ref, gkn_ref,
                     cos_ref, sin_ref,
                     qcat_ref, kcat_ref, vt_ref, ckv_ref, kpe_ref, *, scale):
    h = h_ref[...]
    r = lax.rsqrt(jnp.mean(h * h, axis=-1, keepdims=True) + EPS)
    xn = ((h * r) * gmix_ref[...]).astype(BF16)
    xk = ((h * r) * gkv_ref[...]).astype(BF16)
    cos = cos_ref[...]
    sin = sin_ref[...]
    half = MLA_ROPE // 2

    ckv = _rms(_dot(xk, wdkvc_ref[...]), gkva_ref[...])
    ckv_ref[...] = ckv
    ap = _dot(xk, wdkvp_ref[...])
    kpe = _rope_lanes(_unit_rms(ap, MLA_ROPE) * gkpe_ref[...], cos, sin, half)
    kpe_ref[...] = kpe[:, :MLA_ROPE]
    kpe_b = kpe.astype(BF16)

    cq = _rms(_dot(xn, wdq_ref[...]), gqa_ref[...]).astype(BF16)
    q = _dot(cq, wuq_ref[...])
    cb = ckv.astype(BF16)
    kn = _dot(cb, wuk_ref[...])
    for hd in range(MLA_HEADS):
        c0 = hd * HEAD_PAD
        qn = _unit_rms(q[:, c0:c0 + MLA_NOPE], MLA_NOPE) * gqn_ref[...]
        qp = _unit_rms(q[:, c0 + MLA_NOPE:c0 + HEAD_PAD], MLA_ROPE) * gqp_ref[...]
        qp = _rope_lanes(qp, cos, sin, half)
        qcat_ref[:, c0:c0 + MLA_NOPE] = (qn * scale).astype(BF16)
        qcat_ref[:, c0 + MLA_NOPE:c0 + HEAD_PAD] = (qp * scale).astype(BF16)
        kh = _unit_rms(kn[:, hd * MLA_NOPE:(hd + 1) * MLA_NOPE], MLA_NOPE) * gkn_ref[...]
        kcat_ref[:, c0:c0 + MLA_NOPE] = kh.astype(BF16)
        kcat_ref[:, c0 + MLA_NOPE:c0 + HEAD_PAD] = kpe_b
    vt = _dot_nt(wuvt_ref[...], cb)
    ones = jnp.ones((V_PAD - MLA_DV, vt.shape[1]), BF16)
    for hd in range(MLA_HEADS):
        vt_ref[hd * V_PAD:hd * V_PAD + MLA_DV, :] = vt[hd * MLA_DV:(hd + 1) * MLA_DV, :].astype(BF16)
        vt_ref[hd * V_PAD + MLA_DV:(hd + 1) * V_PAD, :] = ones


def _mla_proj(h, weights, cos_tab, sin_tab, n_prompt_tiles, tiles_per_seq, scale):
    n, d = h.shape
    tm = TOKEN_TILE
    hc = MLA_HEADS * HEAD_PAD
    hn = MLA_HEADS * V_PAD
    kv_lora = weights[7].shape[1]

    def tab_map(i):
        return (jnp.where(i < n_prompt_tiles, i % tiles_per_seq, tiles_per_seq), 0)

    row = lambda i: (i, 0)
    in_specs = [pl.BlockSpec((tm, d), row)] + [_resident(w.shape) for w in weights]
    in_specs += [pl.BlockSpec((tm, LANES), tab_map), pl.BlockSpec((tm, LANES), tab_map)]
    return pl.pallas_call(
        functools.partial(_mla_proj_kernel, scale=scale),
        grid=(n // tm,),
        in_specs=in_specs,
        out_specs=[
            pl.BlockSpec((tm, hc), row),
            pl.BlockSpec((tm, hc), row),
            pl.BlockSpec((hn, tm), lambda i: (0, i)),
            pl.BlockSpec((tm, kv_lora), row),
            pl.BlockSpec((tm, MLA_ROPE), row),
        ],
        out_shape=[
            jax.ShapeDtypeStruct((n, hc), BF16),
            jax.ShapeDtypeStruct((n, hc), BF16),
            jax.ShapeDtypeStruct((hn, n), BF16),
            jax.ShapeDtypeStruct((n, kv_lora), F32),
            jax.ShapeDtypeStruct((n, MLA_ROPE), F32),
        ],
        compiler_params=_params(("parallel",), 48),
        name="mla_proj",
    )(h, *weights, cos_tab, sin_tab)


def _prompt_attn_kernel(qi_ref, ki_ref, q_ref, k_ref, vt_ref, o_ref, *scratch, tile, qblock):
    m_refs = scratch[:MLA_HEADS]
    acc_refs = scratch[MLA_HEADS:2 * MLA_HEADS]
    s_scr = scratch[2 * MLA_HEADS:2 * MLA_HEADS + 2]
    mx_scr = scratch[2 * MLA_HEADS + 2:]
    qi = qi_ref[pl.program_id(1)]
    ki = ki_ref[pl.program_id(1)]

    @pl.when(ki == 0)
    def _():
        for h in range(MLA_HEADS):
            m_refs[h][...] = jnp.full_like(m_refs[h], -jnp.inf)
            acc_refs[h][...] = jnp.zeros_like(acc_refs[h])

    def key_tile(diagonal):
        if diagonal:
            key = lax.broadcasted_iota(jnp.int32, (tile, tile), 0)
            qry = lax.broadcasted_iota(jnp.int32, (tile, tile), 1)
        def scores(h):
            cols = slice(h * HEAD_PAD, (h + 1) * HEAD_PAD)
            s = _dot_nt(k_ref[:, cols], q_ref[:, cols])
            if diagonal:
                s = jnp.where(key <= qry, s, -jnp.inf)
            s_scr[h % 2][...] = s
            mx_scr[h % 2][...] = jnp.max(s, axis=0, keepdims=True)

        def fold(h):
            m_old = m_refs[h][...]
            m_new = jnp.maximum(m_old, mx_scr[h % 2][...])
            m_refs[h][...] = m_new
            p = jnp.exp2(s_scr[h % 2][...] - m_new).astype(BF16)
            acc_refs[h][...] = jnp.exp2(m_old - m_new) * acc_refs[h][...] + _dot(
                vt_ref[h * V_PAD:(h + 1) * V_PAD, :], p)

        scores(0)
        for h in range(MLA_HEADS):
            if h + 1 < MLA_HEADS:
                scores(h + 1)
            fold(h)

    @pl.when(ki < qi)
    def _():
        key_tile(False)

    @pl.when(ki == qi)
    def _():
        key_tile(True)
        outs = []
        for h in range(MLA_HEADS):
            blk = acc_refs[h][...]
            outs.append(blk[:MLA_DV] / blk[MLA_DV:MLA_DV + 1])
        o_ref[...] = jnp.concatenate(outs, axis=0).T


def _prompt_attn(qcat, kcat, vt, batch, seq):
    t = ATTN_TILE
    per_seq = seq // t
    hc = qcat.shape[1]
    hn = MLA_HEADS * MLA_DV
    hvp = vt.shape[0]
    pairs = [(qi, ki) for qi in range(per_seq) for ki in range(qi + 1)]
    qi_tab = jnp.asarray([p[0] for p in pairs], jnp.int32)
    ki_tab = jnp.asarray([p[1] for p in pairs], jnp.int32)
    qrow = lambda b, s, qt, kt: (b * per_seq + qt[s], 0)
    krow = lambda b, s, qt, kt: (b * per_seq + kt[s], 0)
    vcol = lambda b, s, qt, kt: (0, b * per_seq + kt[s])
    grid_spec = pltpu.PrefetchScalarGridSpec(
        num_scalar_prefetch=2,
        grid=(batch, len(pairs)),
        in_specs=[
            pl.BlockSpec((t, hc), qrow),
            pl.BlockSpec((t, hc), krow),
            pl.BlockSpec((hvp, t), vcol),
        ],
        out_specs=pl.BlockSpec((t, hn), qrow),
        scratch_shapes=([pltpu.VMEM((1, t), F32)] * MLA_HEADS + [pltpu.VMEM((V_PAD, t), F32)] * MLA_HEADS
                        + [pltpu.VMEM((t, t), F32)] * 2 + [pltpu.VMEM((1, t), F32)] * 2),
    )
    return pl.pallas_call(
        functools.partial(_prompt_attn_kernel, tile=t, qblock=ATTN_QBLOCK),
        grid_spec=grid_spec,
        out_shape=jax.ShapeDtypeStruct((batch * seq, hn), F32),
        compiler_params=_params(("parallel", "arbitrary"), 40),
        name="prompt_attn",
    )(qi_tab, ki_tab, qcat, kcat, vt)


def _absorb_q_kernel(qcat_ref, gkn_ref, wuk_ref, qa_ref, qps_ref):
    lora = wuk_ref.shape[0]
    for h in range(MLA_HEADS):
        c0 = h * HEAD_PAD
        qg = (qcat_ref[:, c0:c0 + MLA_NOPE].astype(F32) * gkn_ref[...]).astype(BF16)
        qa_ref[:, h * lora:(h + 1) * lora] = _dot_nt(qg, wuk_ref[:, h * MLA_NOPE:(h + 1) * MLA_NOPE])
        qps_ref[:, h * LANES:(h + 1) * LANES] = qcat_ref[:, c0 + MLA_NOPE:c0 + HEAD_PAD].astype(F32)


def _absorb_q(qcat, gkn, wuk, n_prompt, n_sample):
    hc = qcat.shape[1]
    lora = wuk.shape[0]
    tm = min(256, n_sample)
    base = n_prompt // tm
    return pl.pallas_call(
        _absorb_q_kernel,
        grid=(n_sample // tm,),
        in_specs=[
            pl.BlockSpec((tm, hc), lambda i: (base + i, 0)),
            _resident(gkn.shape),
            _resident(wuk.shape),
        ],
        out_specs=[
            pl.BlockSpec((tm, MLA_HEADS * lora), lambda i: (i, 0)),
            pl.BlockSpec((tm, MLA_HEADS * LANES), lambda i: (i, 0)),
        ],
        out_shape=[
            jax.ShapeDtypeStruct((n_sample, MLA_HEADS * lora), F32),
            jax.ShapeDtypeStruct((n_sample, MLA_HEADS * LANES), F32),
        ],
        compiler_params=_params(("parallel",), 32),
        name="absorb_q",
    )(qcat, gkn, wuk)


def _sample_attn_kernel(pt_ref, qa_ref, qp_ref, cnew_ref, pnew_ref, wukt_ref, wuv_ref, cache_c_ref, cache_pt_ref,
                        o_ref, cbuf, pbuf, sem, m_ref, l_ref, acc_ref, lhs_s, qa_s, qp_s, cnew_s, pnew_s,
                        cbb0, cbb1, s_scr0, s_scr1, *, n_pages, ppc, page, dec_seq, lora, key_block):
    b = pl.program_id(0)
    nb = pl.num_programs(0)
    n_chunks = n_pages // ppc
    hq = MLA_HEADS * dec_seq
    n_proj = MLA_HEADS * MLA_NOPE

    def copies(seq, chunk, slot):
        out = []
        for p in range(ppc):
            pg = pt_ref[seq, chunk * ppc + p]
            out.append(pltpu.make_async_copy(cache_c_ref.at[pg], cbuf.at[slot, pl.ds(p * page, page)], sem.at[0, slot]))
            out.append(pltpu.make_async_copy(cache_pt_ref.at[pg], pbuf.at[slot, :, pl.ds(p * page, page)],
                                             sem.at[1, slot]))
        return out

    @pl.when(b == 0)
    def _():
        for cp in copies(0, 0, 0):
            cp.start()
        lhs_s[0:n_proj, :] = wukt_ref[...]

    for h in range(MLA_HEADS):
        qa_s[h * dec_seq:(h + 1) * dec_seq, :] = qa_ref[:, h * lora:(h + 1) * lora]
        qp_s[h * dec_seq:(h + 1) * dec_seq, :] = qp_ref[:, h * LANES:(h + 1) * LANES]
    lhs_s[n_proj:n_proj + hq, :] = qa_s[...].astype(BF16)
    qp = qp_s[...][:, :MLA_ROPE].astype(BF16)
    m_ref[...] = jnp.full_like(m_ref, -jnp.inf)
    l_ref[...] = jnp.zeros_like(l_ref)
    acc_ref[...] = jnp.zeros_like(acc_ref)

    def latent_scores(cb):
        t = cb.shape[0]
        kb = min(key_block, t)
        blocks = []
        for j in range(t // kb):
            kt = _dot_nt(lhs_s[...], cb[j * kb:(j + 1) * kb, :])
            parts = []
            for h in range(MLA_HEADS):
                kh = kt[h * MLA_NOPE:(h + 1) * MLA_NOPE, :]
                ms = jnp.sum(kh * kh, axis=0, keepdims=True) * (1.0 / MLA_NOPE)
                parts.append(kt[n_proj + h * dec_seq:n_proj + (h + 1) * dec_seq, :] * lax.rsqrt(ms + EPS))
            blocks.append(jnp.concatenate(parts, axis=0))
        return blocks[0] if len(blocks) == 1 else jnp.concatenate(blocks, axis=1)

    cbb = (cbb0, cbb1)
    s_scr = (s_scr0, s_scr1)

    def project(slot, par):
        cb = cbuf[slot].astype(BF16)
        cbb[par][...] = cb
        s_scr[par][...] = latent_scores(cb) + _dot(qp, pbuf[slot].astype(BF16))

    def absorb(s, cb):
        m_old = m_ref[...]
        m_new = jnp.maximum(m_old, jnp.max(s, axis=-1, keepdims=True))
        alpha = jnp.exp2(m_old - m_new)
        p = jnp.exp2(s - m_new)
        l_ref[...] = alpha * l_ref[...] + jnp.sum(p, axis=-1, keepdims=True)
        m_ref[...] = m_new
        acc_ref[...] = alpha * acc_ref[...] + _dot(p.astype(BF16), cb)

    def fetch_next(c, slot):
        if c + 1 < n_chunks:
            for cp in copies(b, c + 1, 1 - slot):
                cp.start()
        else:
            @pl.when(b + 1 < nb)
            def _():
                for cp in copies(b + 1, 0, 1 - slot):
                    cp.start()

        for cp in copies(b, c, slot):
            cp.wait()

    for c in range(n_chunks):
        slot = (b * n_chunks + c) % 2 if n_chunks % 2 else c % 2
        fetch_next(c, slot)
        project(slot, c % 2)
        if c:
            absorb(s_scr[(c - 1) % 2][...], cbb[(c - 1) % 2][...])

    cnew_s[...] = jnp.zeros_like(cnew_s)
    pnew_s[...] = jnp.zeros_like(pnew_s)
    cnew_s[0:dec_seq, :] = cnew_ref[...]
    pnew_s[0:dec_seq, :] = pnew_ref[...]
    cnew = cnew_s[...].astype(BF16)
    row_tok = lax.broadcasted_iota(jnp.int32, (hq, LANES), 0) % dec_seq
    col_tok = lax.broadcasted_iota(jnp.int32, (hq, LANES), 1)
    s_new = latent_scores(cnew) + _dot_nt(qp, pnew_s[...].astype(BF16))
    s_new = jnp.where(col_tok <= row_tok, s_new, -jnp.inf)
    last = (n_chunks - 1) % 2
    absorb(jnp.concatenate([s_scr[last][...], s_new], axis=1), jnp.concatenate([cbb[last][...], cnew], axis=0))

    lat = (acc_ref[...] / l_ref[...]).astype(BF16)
    full = _dot(lat, wuv_ref[...])
    for h in range(MLA_HEADS):
        cols = slice(h * MLA_DV, (h + 1) * MLA_DV)
        o_ref[:, cols] = full[h * dec_seq:(h + 1) * dec_seq, cols]


def _sample_attn(page_table, qa, qp, ckv, kpe, wukt, wuv, cache_c, cache_pt, n_prompt, dec_batch, dec_seq):
    n_pages = page_table.shape[1]
    page = cache_c.shape[1]
    lora = cache_c.shape[2]
    ppc = min(PAGES_PER_CHUNK, n_pages)
    hn = MLA_HEADS * MLA_DV
    hq = MLA_HEADS * dec_seq
    base = n_prompt // dec_seq
    srow = lambda b, pt: (b, 0)
    nrow = lambda b, pt: (base + b, 0)
    grid_spec = pltpu.PrefetchScalarGridSpec(
        num_scalar_prefetch=1,
        grid=(dec_batch,),
        in_specs=[
            pl.BlockSpec((dec_seq, MLA_HEADS * lora), srow),
            pl.BlockSpec((dec_seq, MLA_HEADS * LANES), srow),
            pl.BlockSpec((dec_seq, lora), nrow),
            pl.BlockSpec((dec_seq, MLA_ROPE), nrow),
            pl.BlockSpec(wukt.shape, lambda b, pt: (0, 0), pipeline_mode=pl.Buffered(1)),
            pl.BlockSpec(wuv.shape, lambda b, pt: (0, 0), pipeline_mode=pl.Buffered(1)),
            pl.BlockSpec(memory_space=pl.ANY),
            pl.BlockSpec(memory_space=pl.ANY),
        ],
        out_specs=pl.BlockSpec((dec_seq, hn), srow),
        scratch_shapes=[
            pltpu.VMEM((2, ppc * page, lora), F32),
            pltpu.VMEM((2, MLA_ROPE, ppc * page), F32),
            pltpu.SemaphoreType.DMA((2, 2)),
            pltpu.VMEM((hq, 1), F32),
            pltpu.VMEM((hq, 1), F32),
            pltpu.VMEM((hq, lora), F32),
            pltpu.VMEM((MLA_HEADS * MLA_NOPE + hq, lora), BF16),
            pltpu.VMEM((hq, lora), F32),
            pltpu.VMEM((hq, LANES), F32),
            pltpu.VMEM((LANES, lora), F32),
            pltpu.VMEM((LANES, MLA_ROPE), F32),
            pltpu.VMEM((ppc * page, lora), BF16),
            pltpu.VMEM((ppc * page, lora), BF16),
            pltpu.VMEM((hq, ppc * page), F32),
            pltpu.VMEM((hq, ppc * page), F32),
        ],
    )
    return pl.pallas_call(
        functools.partial(_sample_attn_kernel, n_pages=n_pages, ppc=ppc, page=page, dec_seq=dec_seq, lora=lora,
                          key_block=SAMPLE_KEY_BLOCK),
        grid_spec=grid_spec,
        out_shape=jax.ShapeDtypeStruct((dec_batch * dec_seq, hn), F32),
        compiler_params=_params(("arbitrary",), 40),
        name="sample_attn",
    )(page_table, qa, qp, ckv, kpe, wukt, wuv, cache_c, cache_pt)


def _route_kernel(h_ref, op_ref, os_ref, wo_ref, gffn_ref, wr_ref, tri_ref, h3_ref, xf_ref, info_ref, infot_ref,
                  cnt_ref, *rest, n_prompt_tiles):
    carry_ref = rest[-1]
    i = pl.program_id(0)

    @pl.when(i == 0)
    def _():
        carry_ref[...] = jnp.zeros_like(carry_ref)

    for zero_ref in rest[:-1]:
        zero_ref[...] = jnp.zeros_like(zero_ref)

    o = jnp.where(i < n_prompt_tiles, op_ref[...], os_ref[...])
    h3 = h_ref[...] + _dot(o.astype(BF16), wo_ref[...])
    h3_ref[...] = h3
    xf = _rms(h3, gffn_ref[...])
    xf_ref[...] = xf
    logits = _dot(xf.astype(BF16), wr_ref[...])
    lane = lax.broadcasted_iota(jnp.int32, logits.shape, 1).astype(F32)
    lg = jnp.where(lane < N_EXPERTS, logits, -jnp.inf)
    v1 = jnp.max(lg, axis=-1, keepdims=True)
    i1 = jnp.min(jnp.where(lg == v1, lane, float(LANES)), axis=-1, keepdims=True)
    lg2 = jnp.where(lane == i1, -jnp.inf, lg)
    v2 = jnp.max(lg2, axis=-1, keepdims=True)
    i2 = jnp.min(jnp.where(lg2 == v2, lane, float(LANES)), axis=-1, keepdims=True)
    e = jnp.exp(v2 - v1)
    g1 = 1.0 / (1.0 + e)
    g2 = e / (1.0 + e)
    oh1 = lane == i1
    oh2 = lane == i2
    chosen = jnp.where(oh1 | oh2, 1.0, 0.0)
    cum = _dot(tri_ref[...], chosen.astype(BF16))
    before = cum - chosen + carry_ref[...]
    pos1 = jnp.sum(jnp.where(oh1, before, 0.0), axis=-1, keepdims=True)
    pos2 = jnp.sum(jnp.where(oh2, before, 0.0), axis=-1, keepdims=True)
    carry_ref[...] = carry_ref[...] + cum[cum.shape[0] - 1:, :]
    cnt_ref[...] = carry_ref[...]
    info = jnp.where(lane == 0, i1, 0.0)
    info = jnp.where(lane == 1, i2, info)
    info = jnp.where(lane == 2, pos1, info)
    info = jnp.where(lane == 3, pos2, info)
    info = jnp.where(lane == 4, g1, info)
    info = jnp.where(lane == 5, g2, info)
    info_ref[...] = info
    infot_ref[...] = info.T[:8, :]


def _route(h2, o_attn_p, o_attn_s, w_o, g_ffn, w_r, tri, sorted_rows):
    n, d = h2.shape
    tm = TOKEN_TILE
    npt = o_attn_p.shape[0] // tm
    hn = o_attn_p.shape[1]
    row = lambda i: (i, 0)
    steps = n // tm
    zc = 0
    for c in range(steps, 0, -1):
        if sorted_rows % c == 0 and (sorted_rows // c) % 8 == 0 and (sorted_rows // c) * d * 4 <= CAST_SLAB_BYTES:
            zc = c
            break
    zero_specs = [pl.BlockSpec((sorted_rows // zc, d), lambda i: (jnp.minimum(i, zc - 1), 0))] if zc else []
    zero_shapes = [jax.ShapeDtypeStruct((sorted_rows, d), F32)] if zc else []
    outs = pl.pallas_call(
        functools.partial(_route_kernel, n_prompt_tiles=npt),
        grid=(steps,),
        in_specs=[
            pl.BlockSpec((tm, d), row),
            pl.BlockSpec((tm, hn), lambda i: (jnp.minimum(i, npt - 1), 0)),
            pl.BlockSpec((tm, hn), lambda i: (jnp.maximum(i - npt, 0), 0)),
            _resident(w_o.shape),
            _resident((1, d)),
            _resident(w_r.shape),
            _resident(tri.shape),
        ],
        out_specs=[
            pl.BlockSpec((tm, d), row),
            pl.BlockSpec((tm, d), row),
            pl.BlockSpec((tm, LANES), row),
            pl.BlockSpec((8, tm), lambda i: (0, i)),
            pl.BlockSpec((1, LANES), lambda i: (0, 0)),
        ] + zero_specs,
        out_shape=[
            jax.ShapeDtypeStruct((n, d), F32),
            jax.ShapeDtypeStruct((n, d), F32),
            jax.ShapeDtypeStruct((n, LANES), F32),
            jax.ShapeDtypeStruct((8, n), F32),
            jax.ShapeDtypeStruct((1, LANES), F32),
        ] + zero_shapes,
        scratch_shapes=[pltpu.VMEM((1, LANES), F32)],
        compiler_params=_params(("arbitrary",), 48),
        name="route",
    )(h2, o_attn_p, o_attn_s, w_o, g_ffn, w_r, tri)
    return tuple(outs[:5]) + ((outs[5] if zc else None),)


def _scatter_kernel(dest_ref, x_ref, xs_in_ref, *rest, tm):
    del xs_in_ref
    n_w = (len(rest) - 2) // 2
    w_in, xs_ref, w_out, sem = rest[:n_w], rest[n_w], rest[n_w + 1:2 * n_w + 1], rest[-1]

    def issue(r, carry):
        for k in range(2):
            d = dest_ref[0, 0, k * tm + r]
            pltpu.make_async_copy(x_ref.at[pl.ds(r, 1)], xs_ref.at[pl.ds(d, 1)], sem).start(priority=k)
        return carry

    lax.fori_loop(0, tm, issue, 0, unroll=8)

    for src, dst in zip(w_in, w_out):
        dst[...] = src[...].astype(dst.dtype)

    def drain(r, carry):
        for k in range(2):
            pltpu.make_async_copy(x_ref.at[pl.ds(r, 1)], xs_ref.at[pl.ds(0, 1)], sem).wait()
        return carry

    lax.fori_loop(0, tm, drain, 0, unroll=8)


CAST_SLAB_BYTES = 8 * MIB


def _cast_steps(steps, mats):
    for c in range(steps, 0, -1):
        if all(m.shape[0] % c == 0 and (m.shape[0] // c) % 16 == 0
               and (m.shape[0] // c) * m.shape[1] * 4 <= CAST_SLAB_BYTES for m in mats):
            return c
    return 0


def _scatter_rows(dest, xf, xs0, mats):
    n, d = xf.shape
    tm = TOKEN_TILE
    steps = n // tm
    c = _cast_steps(steps, mats)
    if c == 0:
        fused = []
    else:
        fused = list(mats)
    slab = lambda i: (jnp.minimum(i, c - 1), 0)
    w_specs = [pl.BlockSpec((m.shape[0] // c, m.shape[1]), slab) for m in fused]
    outs = pl.pallas_call(
        functools.partial(_scatter_kernel, tm=tm),
        grid=(steps,),
        in_specs=[
            pl.BlockSpec((1, 1, 2 * tm), lambda i: (i, 0, 0), memory_space=pltpu.SMEM),
            pl.BlockSpec((tm, d), lambda i: (i, 0)),
            pl.BlockSpec(memory_space=pl.ANY),
        ] + w_specs,
        out_specs=[pl.BlockSpec(memory_space=pl.ANY)] + w_specs,
        out_shape=[jax.ShapeDtypeStruct(xs0.shape, xs0.dtype)]
        + [jax.ShapeDtypeStruct(m.shape, BF16) for m in fused],
        scratch_shapes=[pltpu.SemaphoreType.DMA(())],
        input_output_aliases={2: 0},
        compiler_params=_params(("arbitrary",), 48),
        name="scatter_rows",
    )(dest, xf, xs0, *fused)
    cast = list(outs[1:]) if fused else [m.astype(BF16) for m in mats]
    return outs[0], cast


def _experts_kernel(te_ref, tr_ref, x_ref, wg_ref, wu_ref, wd_ref, o_ref, acc_ref, xb_ref, act0, act1, *, sub):
    t = pl.program_id(0)
    c = pl.program_id(1)
    n_sub = x_ref.shape[0] // sub
    act_s = (act0, act1)

    @pl.when(c == 0)
    def _():
        xb_ref[...] = x_ref[...].astype(BF16)
        acc_ref[...] = jnp.zeros_like(acc_ref)

    @pl.when(tr_ref[t] > 0)
    def _():
        def up(i):
            xb = xb_ref[i * sub:(i + 1) * sub, :]
            act_s[i % 2][...] = (_silu(_dot(xb, wg_ref[0])) * _dot(xb, wu_ref[0])).astype(BF16)

        def down(i):
            acc_ref[i * sub:(i + 1) * sub, :] += _dot(act_s[i % 2][...], wd_ref[0])

        up(0)
        for i in range(n_sub):
            if i + 1 < n_sub:
                up(i + 1)
            down(i)

    @pl.when(c == pl.num_programs(1) - 1)
    def _():
        o_ref[...] = acc_ref[...]


def _experts(tile_expert, tile_rows, xs, w_gu, w_down):
    rows, d = xs.shape
    tm = EXPERT_TILE
    fc = FF_CHUNK
    d_ff = w_down.shape[1]
    n_c = d_ff // fc
    n_tiles = rows // tm

    def ceff(t, c, tr):
        return jnp.where(tr[t] > 0, c, n_c - 1)

    grid_spec = pltpu.PrefetchScalarGridSpec(
        num_scalar_prefetch=2,
        grid=(n_tiles, n_c),
        in_specs=[
            pl.BlockSpec((tm, d), lambda t, c, te, tv: (t, 0)),
            pl.BlockSpec((1, d, fc), lambda t, c, te, tv: (te[t], 0, ceff(t, c, tv))),
            pl.BlockSpec((1, d, fc), lambda t, c, te, tv: (te[t], 0, n_c + ceff(t, c, tv))),
            pl.BlockSpec((1, fc, d), lambda t, c, te, tv: (te[t], ceff(t, c, tv), 0)),
        ],
        out_specs=pl.BlockSpec((tm, d), lambda t, c, te, tv: (t, 0)),
        scratch_shapes=[pltpu.VMEM((tm, d), F32), pltpu.VMEM((tm, d), BF16),
                        pltpu.VMEM((EXPERT_SUB, fc), BF16), pltpu.VMEM((EXPERT_SUB, fc), BF16)],
    )
    return pl.pallas_call(
        functools.partial(_experts_kernel, sub=EXPERT_SUB),
        grid_spec=grid_spec,
        out_shape=jax.ShapeDtypeStruct((rows, d), F32),
        compiler_params=_params(("arbitrary", "arbitrary"), 52),
        name="experts",
    )(tile_expert, tile_rows, xs, w_gu, w_gu, w_down)


def _combine_kernel(dest_ref, h_ref, info_ref, ys_ref, yp_ref, ysm_ref, buf, sem, *, tm, n_prompt_tiles):
    def issue(r, carry):
        for k in range(2):
            d = dest_ref[0, 0, k * tm + r]
            pltpu.make_async_copy(ys_ref.at[pl.ds(d, 1)], buf.at[k, pl.ds(r, 1)], sem).start(priority=k)
        return carry

    lax.fori_loop(0, tm, issue, 0, unroll=8)

    def drain(r, carry):
        for k in range(2):
            pltpu.make_async_copy(ys_ref.at[pl.ds(0, 1)], buf.at[k, pl.ds(r, 1)], sem).wait()
        return carry

    lax.fori_loop(0, tm, drain, 0, unroll=8)
    info = info_ref[...]
    y = h_ref[...] + info[:, 4:5] * buf[0] + info[:, 5:6] * buf[1]

    @pl.when(pl.program_id(0) < n_prompt_tiles)
    def _():
        yp_ref[...] = y

    @pl.when(pl.program_id(0) >= n_prompt_tiles)
    def _():
        ysm_ref[...] = y


def _combine(dest, h3, info, ys, n_prompt):
    n, d = h3.shape
    tm = TOKEN_TILE
    npt = n_prompt // tm
    return pl.pallas_call(
        functools.partial(_combine_kernel, tm=tm, n_prompt_tiles=npt),
        grid=(n // tm,),
        in_specs=[
            pl.BlockSpec((1, 1, 2 * tm), lambda i: (i, 0, 0), memory_space=pltpu.SMEM),
            pl.BlockSpec((tm, d), lambda i: (i, 0)),
            pl.BlockSpec((tm, LANES), lambda i: (i, 0)),
            pl.BlockSpec(memory_space=pl.ANY),
        ],
        out_specs=[
            pl.BlockSpec((tm, d), lambda i: (jnp.minimum(i, npt - 1), 0)),
            pl.BlockSpec((tm, d), lambda i: (jnp.maximum(i - npt, 0), 0)),
        ],
        out_shape=[
            jax.ShapeDtypeStruct((n_prompt, d), F32),
            jax.ShapeDtypeStruct((n - n_prompt, d), F32),
        ],
        scratch_shapes=[pltpu.VMEM((2, tm, d), F32), pltpu.SemaphoreType.DMA(())],
        compiler_params=_params(("arbitrary",), 32),
        name="combine",
    )(dest, h3, info, ys)


def _rope_tables(pos, dim):
    inv = ROPE_BASE ** (-jnp.arange(0, dim, 2, dtype=F32) / dim)
    ang = pos.astype(F32)[:, None] * inv[None, :]
    return jnp.cos(ang), jnp.sin(ang)


def kernel(x_prompt, x_sample, state_ret, cache_ckv, cache_kpe, page_table, g_pre_mix, g_pre_ffn, ret_w_in, ret_g_out, ret_w_out, mla_g_kv_in, mla_w_dkv, mla_g_kv_a, mla_g_kpe, mla_w_uk, mla_w_uv, mla_g_kn, mla_w_dq, mla_g_qa, mla_w_uq, mla_g_qn, mla_g_qp, mla_w_o, ffn_w_gu, ffn_w_down, moe_w_router, moe_w_gu, moe_w_down):
    batch, seq, d = x_prompt.shape
    dec_batch, dec_seq, _ = x_sample.shape
    n_prompt = batch * seq
    n_sample = dec_batch * dec_seq
    n = n_prompt + n_sample
    tm = TOKEN_TILE
    past = page_table.shape[1] * cache_ckv.shape[1]
    hv = ret_g_out.shape[1]
    hk = (ret_w_in.shape[2] - 2 * hv) // 2
    kv_lora = mla_g_kv_a.shape[0]
    assert seq % tm == 0 and n_sample % tm == 0 and seq % ATTN_TILE == 0 and seq % RET_ROWS == 0
    n_prompt_tiles = n_prompt // tm
    tiles_per_seq = seq // tm

    x_p = x_prompt.reshape(n_prompt, d)
    x_s = x_sample.reshape(n_sample, d)

    pos = jnp.concatenate([jnp.arange(seq), past + (jnp.arange(tm) % dec_seq)])
    cos_r, sin_r = _rope_tables(pos, hk // RET_HEADS)
    cos_m, sin_m = _rope_tables(pos, MLA_ROPE)
    cos_m = jnp.tile(cos_m, (1, 2 * LANES // MLA_ROPE))
    sin_m = jnp.tile(jnp.concatenate([-sin_m, sin_m], axis=-1), (1, LANES // MLA_ROPE))

    q, k, v, gate = _ret_in(x_p, x_s, g_pre_mix[0][None], ret_w_in[0].astype(BF16), cos_r, sin_r,
                            n_prompt_tiles, tiles_per_seq, hk, hv)
    onorm_p, s_prompt = _ret_prompt(q, k, v, _decay_consts(RET_CHUNK, RET_CHUNK), batch, seq)
    seqs = 16 // dec_seq if dec_seq < 16 else 1
    onorm_s, s_sample = _ret_sample(q, k, v, state_ret[0], _decay_consts(dec_seq, seqs * dec_seq),
                                    n_prompt, dec_batch, dec_seq)
    h2 = _mix_ffn(x_p, x_s, onorm_p, onorm_s, gate, ret_g_out[0][None], ret_w_out[0].astype(BF16),
                  g_pre_ffn[0][None], ffn_w_gu[0].astype(BF16), ffn_w_down[0].astype(BF16))

    q_lora = mla_w_dq.shape[2]
    head_dim = MLA_NOPE + MLA_ROPE
    w_uq = jnp.pad(mla_w_uq[0].reshape(q_lora, MLA_HEADS, head_dim), ((0, 0), (0, 0), (0, HEAD_PAD - head_dim)))
    w_uq = w_uq.reshape(q_lora, MLA_HEADS * HEAD_PAD).astype(BF16)
    w_dkv_c = mla_w_dkv[:, :kv_lora].astype(BF16)
    w_dkv_p = jnp.pad(mla_w_dkv[:, kv_lora:], ((0, 0), (0, LANES - MLA_ROPE))).astype(BF16)
    pad_rope = lambda g: jnp.pad(g, (0, LANES - MLA_ROPE))[None]
    weights = [
        g_pre_mix[1][None], mla_g_kv_in[None], mla_w_dq[0].astype(BF16), mla_g_qa[0][None], w_uq,
        mla_g_qn[0][None], pad_rope(mla_g_qp[0]),
        w_dkv_c, w_dkv_p, mla_g_kv_a[None], pad_rope(mla_g_kpe),
        mla_w_uk.astype(BF16), mla_w_uv.T.astype(BF16), mla_g_kn[None],
    ]
    scale = float(head_dim ** -0.5 * 1.4426950408889634)
    qcat, kcat, vt, ckv, kpe = _mla_proj(h2, weights, cos_m, sin_m, n_prompt_tiles, tiles_per_seq, scale)
    o_attn_p = _prompt_attn(qcat, kcat, vt, batch, seq)
    qa, qps = _absorb_q(qcat, mla_g_kn[None], mla_w_uk.astype(BF16), n_prompt, n_sample)
    o_attn_s = _sample_attn(page_table, qa, qps, ckv, kpe, mla_w_uk.T.astype(BF16), mla_w_uv.astype(BF16),
                            cache_ckv, cache_kpe.transpose(0, 2, 1), n_prompt, dec_batch, dec_seq)

    w_r = jnp.pad(moe_w_router[0], ((0, 0), (0, LANES - N_EXPERTS))).astype(BF16)
    tri = (jnp.arange(tm)[:, None] >= jnp.arange(tm)[None, :]).astype(BF16)
    te = EXPERT_TILE
    n_tiles = (2 * n) // te + N_EXPERTS
    h3, xf, info, info_t, counts, xs0 = _route(h2, o_attn_p, o_attn_s, mla_w_o[0].astype(BF16),
                                               g_pre_ffn[1][None], w_r, tri, n_tiles * te)
    if xs0 is None:
        xs0 = jnp.zeros((n_tiles * te, d), F32)

    counts = counts[0, :N_EXPERTS].astype(jnp.int32)
    tiles_e = (counts + te - 1) // te
    tile_end = jnp.cumsum(tiles_e)
    offsets = (tile_end - tiles_e) * te
    tile_ids = jnp.arange(n_tiles, dtype=jnp.int32)
    tile_valid = tile_ids < tile_end[-1]
    tile_expert = jnp.minimum(jnp.sum(tile_ids[:, None] >= tile_end[None, :], axis=1), N_EXPERTS - 1)
    first_tile = jnp.zeros_like(tile_ids)
    count_of = jnp.zeros_like(tile_ids)
    for e in range(N_EXPERTS):
        first_tile = jnp.where(tile_expert == e, tile_end[e] - tiles_e[e], first_tile)
        count_of = jnp.where(tile_expert == e, counts[e], count_of)
    tile_rows = jnp.where(tile_valid, jnp.clip(count_of - (tile_ids - first_tile) * te, 0, te), 0).astype(jnp.int32)
    last_expert = jnp.sum((tile_end[-1] - 1) >= tile_end).astype(jnp.int32)
    tile_expert = jnp.where(tile_valid, tile_expert, last_expert).astype(jnp.int32)
    idx = info_t[0:2].astype(jnp.int32)
    base = jnp.zeros_like(idx)
    for e in range(N_EXPERTS):
        base = jnp.where(idx == e, offsets[e], base)
    dest = base + info_t[2:4].astype(jnp.int32)
    dest = dest.reshape(2, n // tm, tm).transpose(1, 0, 2).reshape(n // tm, 1, 2 * tm)

    w_gu, w_down = moe_w_gu[0], moe_w_down[0]
    xs, (w_gu_b, w_down_b) = _scatter_rows(dest, xf, xs0,
                                           [w_gu.reshape(-1, w_gu.shape[-1]), w_down.reshape(-1, w_down.shape[-1])])
    ys = _experts(tile_expert, tile_rows, xs, w_gu_b.reshape(w_gu.shape), w_down_b.reshape(w_down.shape))
    y_p, y_s = _combine(dest, h3, info, ys, n_prompt)

    return (
        y_p.reshape(batch, seq, d),
        y_s.reshape(dec_batch, dec_seq, d),
        s_prompt[None],
        s_sample[None],
        ckv[:n_prompt].reshape(batch, seq, kv_lora),
        kpe[:n_prompt].reshape(batch, seq, MLA_ROPE),
        ckv[n_prompt:].reshape(dec_batch, dec_seq, kv_lora),
        kpe[n_prompt:].reshape(dec_batch, dec_seq, MLA_ROPE),
    )
```

```python
import functools

import jax
import jax.numpy as jnp
from jax import lax
from jax.experimental import pallas as pl
from jax.experimental.pallas import tpu as pltpu

F32 = jnp.float32
BF16 = jnp.bfloat16
EPS = 1e-6
ROPE_BASE = 10000.0

RET_HEADS = 4
RET_CHUNK = 128
MLA_HEADS = 8
MLA_NOPE = 128
MLA_ROPE = 64
MLA_DV = 128
N_EXPERTS = 8
LANES = 128

TOKEN_TILE = 512
RET_ROWS = 512
ATTN_TILE = 512
ATTN_QBLOCK = 256
EXPERT_TILE = 1024
EXPERT_SUB = 1024
EXPERT_PART = 256
FF_CHUNK = 512
PAGES_PER_CHUNK = 32
SAMPLE_KEY_BLOCK = 256
MIB = 1024 * 1024


def _params(semantics, vmem_mib, flags=None):
    return pltpu.CompilerParams(dimension_semantics=semantics, vmem_limit_bytes=vmem_mib * MIB, flags=flags)


def _resident(shape):
    return pl.BlockSpec(shape, lambda *_: (0,) * len(shape), pipeline_mode=pl.Buffered(1))


def _rms(x, g):
    ms = jnp.mean(x * x, axis=-1, keepdims=True)
    return (x * lax.rsqrt(ms + EPS)) * g


def _dot(a, b):
    return jnp.dot(a, b, preferred_element_type=F32)


def _dot_nt(a, b):
    return lax.dot_general(a, b, (((1,), (1,)), ((), ())), preferred_element_type=F32)


def _dot_tn(a, b):
    return lax.dot_general(a, b, (((0,), (0,)), ((), ())), preferred_element_type=F32)


def _silu(x):
    return x * jax.nn.sigmoid(x)


def _split_specs(tm, width, n_prompt_tiles):
    return [
        pl.BlockSpec((tm, width), lambda i: (jnp.minimum(i, n_prompt_tiles - 1), 0)),
        pl.BlockSpec((tm, width), lambda i: (jnp.maximum(i - n_prompt_tiles, 0), 0), pipeline_mode=pl.Buffered(1)),
    ]


def _pick(p_ref, s_ref, n_prompt_tiles):
    return jnp.where(pl.program_id(0) < n_prompt_tiles, p_ref[...], s_ref[...])


def _ret_in_kernel(xp_ref, xs_ref, g_ref, w_ref, cos_ref, sin_ref, q_ref, k_ref, v_ref, gate_ref,
                   *, hk, hv, dk, n_prompt_tiles):
    xn = _rms(_pick(xp_ref, xs_ref, n_prompt_tiles), g_ref[...]).astype(BF16)
    cos = cos_ref[...]
    sin = sin_ref[...]
    half = dk // 2

    def rope_store(dst, col, acc, scale):
        x1 = acc[:, :half]
        x2 = acc[:, half:]
        dst[:, col:col + half] = ((x1 * cos - x2 * sin) * scale).astype(BF16)
        dst[:, col + half:col + dk] = ((x1 * sin + x2 * cos) * scale).astype(BF16)

    for h in range(hk // dk):
        rope_store(q_ref, h * dk, _dot(xn, w_ref[:, h * dk:(h + 1) * dk]), 1.0)
    for h in range(hk // dk):
        rope_store(k_ref, h * dk, _dot(xn, w_ref[:, hk + h * dk:hk + (h + 1) * dk]), dk ** -0.5)
    step = 512
    for c in range(hv // step):
        v_ref[:, c * step:(c + 1) * step] = _dot(xn, w_ref[:, 2 * hk + c * step:2 * hk + (c + 1) * step]).astype(BF16)
    for c in range(hv // step):
        gate_ref[:, c * step:(c + 1) * step] = _dot(
            xn, w_ref[:, 2 * hk + hv + c * step:2 * hk + hv + (c + 1) * step]).astype(BF16)


def _ret_in(x_p, x_s, g, w, cos_tab, sin_tab, n_prompt_tiles, tiles_per_seq, hk, hv):
    d = x_p.shape[1]
    n = x_p.shape[0] + x_s.shape[0]
    tm = TOKEN_TILE
    dk = hk // RET_HEADS

    def tab_map(i):
        return (jnp.where(i < n_prompt_tiles, i % tiles_per_seq, tiles_per_seq), 0)

    return pl.pallas_call(
        functools.partial(_ret_in_kernel, hk=hk, hv=hv, dk=dk, n_prompt_tiles=n_prompt_tiles),
        grid=(n // tm,),
        in_specs=_split_specs(tm, d, n_prompt_tiles) + [
            _resident((1, d)),
            _resident(w.shape),
            pl.BlockSpec((tm, dk // 2), tab_map),
            pl.BlockSpec((tm, dk // 2), tab_map),
        ],
        out_specs=[
            pl.BlockSpec((tm, hk), lambda i: (i, 0)),
            pl.BlockSpec((tm, hk), lambda i: (i, 0)),
            pl.BlockSpec((tm, hv), lambda i: (i, 0)),
            pl.BlockSpec((tm, hv), lambda i: (i, 0)),
        ],
        out_shape=[
            jax.ShapeDtypeStruct((n, hk), BF16),
            jax.ShapeDtypeStruct((n, hk), BF16),
            jax.ShapeDtypeStruct((n, hv), BF16),
            jax.ShapeDtypeStruct((n, hv), BF16),
        ],
        compiler_params=_params(("parallel",), 48),
        name="ret_in",
    )(x_p, x_s, g, w, cos_tab, sin_tab)


def _group_norm(o):
    mu = jnp.mean(o, axis=-1, keepdims=True)
    var = jnp.mean(jnp.square(o - mu), axis=-1, keepdims=True)
    return (o - mu) * lax.rsqrt(var + EPS)


def _ret_prompt_kernel(q_ref, k_ref, v_ref, dmat_ref, dq_ref, dk_ref, gc_ref, o_ref, sfin_ref, *state_refs,
                       dk, dv, chunk):
    j = pl.program_id(1)

    @pl.when(j == 0)
    def _():
        for h in range(RET_HEADS):
            state_refs[h][...] = jnp.zeros_like(state_refs[h])

    def one_chunk(c, carry):
        r0 = pl.multiple_of(c * chunk, chunk)
        rows = pl.ds(r0, chunk)

        def first(h):
            q = q_ref[rows, h * dk:(h + 1) * dk]
            k = k_ref[rows, h * dk:(h + 1) * dk]
            v = v_ref[rows, h * dv:(h + 1) * dv]
            s = state_refs[h][...]
            inner = (_dot_nt(q, k) * dmat_ref[h]).astype(BF16)
            cross = _dot(q, s.astype(BF16)) * dq_ref[h]
            kw = (k.astype(F32) * dk_ref[h]).astype(BF16)
            state_refs[h][...] = s * gc_ref[h] + _dot_tn(kw, v)
            return inner, cross, v

        def second(h, inner, cross, v):
            o = _dot(inner, v) + cross
            o_ref[rows, h * dv:(h + 1) * dv] = _group_norm(o).astype(BF16)

        pending = first(0)
        for h in range(RET_HEADS):
            nxt = first(h + 1) if h + 1 < RET_HEADS else None
            second(h, *pending)
            pending = nxt
        return carry

    lax.fori_loop(0, q_ref.shape[0] // chunk, one_chunk, 0)

    @pl.when(j == pl.num_programs(1) - 1)
    def _():
        for h in range(RET_HEADS):
            sfin_ref[0, h] = state_refs[h][...]


def _ret_prompt(q, k, v, consts, batch, seq):
    hk = q.shape[1]
    hv = v.shape[1]
    dk, dv = hk // RET_HEADS, hv // RET_HEADS
    rb = RET_ROWS
    per_seq = seq // rb
    dmat, dq, dkk, gc = consts
    row = lambda b, j: (b * per_seq + j, 0)
    return pl.pallas_call(
        functools.partial(_ret_prompt_kernel, dk=dk, dv=dv, chunk=RET_CHUNK),
        grid=(batch, per_seq),
        in_specs=[
            pl.BlockSpec((rb, hk), row),
            pl.BlockSpec((rb, hk), row),
            pl.BlockSpec((rb, hv), row),
            _resident(dmat.shape),
            _resident(dq.shape),
            _resident(dkk.shape),
            _resident(gc.shape),
        ],
        out_specs=[
            pl.BlockSpec((rb, hv), row),
            pl.BlockSpec((1, RET_HEADS, dk, dv), lambda b, j: (b, 0, 0, 0)),
        ],
        out_shape=[
            jax.ShapeDtypeStruct((batch * seq, hv), BF16),
            jax.ShapeDtypeStruct((batch, RET_HEADS, dk, dv), F32),
        ],
        scratch_shapes=[pltpu.VMEM((dk, dv), F32)] * RET_HEADS,
        compiler_params=_params(("parallel", "arbitrary"), 40),
        name="ret_prompt",
    )(q, k, v, dmat, dq, dkk, gc)


def _ret_sample_kernel(q_ref, k_ref, v_ref, s_ref, dmat_ref, dq_ref, dk_ref, gc_ref,
                       o_ref, snew_ref, *, dk, dv, dec_seq, seqs):
    rows = seqs * dec_seq
    row_seq = lax.broadcasted_iota(jnp.int32, (rows, 1), 0) // dec_seq
    for h in range(RET_HEADS):
        q = q_ref[:, h * dk:(h + 1) * dk]
        k = k_ref[:, h * dk:(h + 1) * dk]
        v = v_ref[:, h * dv:(h + 1) * dv]
        inner = _dot_nt(q, k) * dmat_ref[h]
        o = _dot(inner.astype(BF16), v)
        kw = k.astype(F32) * dk_ref[h]
        cross = jnp.zeros((rows, dv), F32)
        for s in range(seqs):
            st = s_ref[s, h]
            mine = row_seq == s
            cross = cross + jnp.where(mine, _dot(q, st.astype(BF16)), 0.0)
            kws = jnp.where(mine, kw, 0.0).astype(BF16)
            snew_ref[s, h] = st * gc_ref[h] + _dot_tn(kws, v)
        o = o + cross * dq_ref[h]
        o_ref[:, h * dv:(h + 1) * dv] = _group_norm(o).astype(BF16)


def _ret_sample(q, k, v, state, consts, n_prompt, dec_batch, dec_seq):
    hk = q.shape[1]
    hv = v.shape[1]
    dk, dv = hk // RET_HEADS, hv // RET_HEADS
    seqs = 16 // dec_seq if dec_seq < 16 else 1
    rows = seqs * dec_seq
    base = n_prompt // rows
    dmat, dq, dkk, gc = consts
    row = lambda i: (base + i, 0)
    return pl.pallas_call(
        functools.partial(_ret_sample_kernel, dk=dk, dv=dv, dec_seq=dec_seq, seqs=seqs),
        grid=(dec_batch // seqs,),
        in_specs=[
            pl.BlockSpec((rows, hk), row),
            pl.BlockSpec((rows, hk), row),
            pl.BlockSpec((rows, hv), row),
            pl.BlockSpec((seqs, RET_HEADS, dk, dv), lambda i: (i, 0, 0, 0)),
            _resident(dmat.shape),
            _resident(dq.shape),
            _resident(dkk.shape),
            _resident(gc.shape),
        ],
        out_specs=[
            pl.BlockSpec((rows, hv), lambda i: (i, 0)),
            pl.BlockSpec((seqs, RET_HEADS, dk, dv), lambda i: (i, 0, 0, 0)),
        ],
        out_shape=[
            jax.ShapeDtypeStruct((dec_batch * dec_seq, hv), BF16),
            jax.ShapeDtypeStruct(state.shape, F32),
        ],
        compiler_params=_params(("parallel",), 40),
        name="ret_sample",
    )(q, k, v, state, dmat, dq, dkk, gc)


def _decay_consts(chunk, rows):
    log_gamma = jnp.log1p(-jnp.exp2(-5.0 - jnp.arange(RET_HEADS, dtype=F32)))
    idx = jnp.arange(rows)
    pos = (idx % chunk).astype(F32)
    same = (idx[:, None] // chunk) == (idx[None, :] // chunk)
    diff = pos[:, None] - pos[None, :]
    lg = log_gamma[:, None, None]
    dmat = jnp.where(same & (diff >= 0), jnp.exp(lg * jnp.maximum(diff, 0.0)), 0.0)
    dq = jnp.exp(log_gamma[:, None] * (pos + 1.0))[..., None]
    dk = jnp.exp(log_gamma[:, None] * (chunk - 1.0 - pos))[..., None]
    gc = jnp.exp(log_gamma * chunk)[:, None, None]
    return dmat.astype(F32), dq.astype(F32), dk.astype(F32), gc.astype(F32)


def _mix_ffn_kernel(xp_ref, xs_ref, op_ref, os_ref, gate_ref, gout_ref, wout_ref, gffn_ref, wgu_ref, wdown_ref,
                    h_ref, act_ref, *, d_ff, chunk, n_prompt_tiles):
    o = _pick(op_ref, os_ref, n_prompt_tiles).astype(F32) * gout_ref[...]
    z = (_silu(gate_ref[...].astype(F32)) * o).astype(BF16)
    h1 = _pick(xp_ref, xs_ref, n_prompt_tiles) + _dot(z, wout_ref[...])
    xf = _rms(h1, gffn_ref[...]).astype(BF16)
    for c in range(d_ff // chunk):
        g = _dot(xf, wgu_ref[:, c * chunk:(c + 1) * chunk])
        u = _dot(xf, wgu_ref[:, d_ff + c * chunk:d_ff + (c + 1) * chunk])
        act_ref[:, c * chunk:(c + 1) * chunk] = (_silu(g) * u).astype(BF16)
    h_ref[...] = h1 + _dot(act_ref[...], wdown_ref[...])


def _mix_ffn(x_p, x_s, onorm_p, onorm_s, gate, g_out, w_out, g_ffn, w_gu, w_down):
    d = x_p.shape[1]
    n = x_p.shape[0] + x_s.shape[0]
    hv = onorm_p.shape[1]
    d_ff = w_down.shape[0]
    tm = TOKEN_TILE
    npt = onorm_p.shape[0] // tm
    return pl.pallas_call(
        functools.partial(_mix_ffn_kernel, d_ff=d_ff, chunk=256, n_prompt_tiles=npt),
        grid=(n // tm,),
        in_specs=_split_specs(tm, d, npt) + _split_specs(tm, hv, npt) + [
            pl.BlockSpec((tm, hv), lambda i: (i, 0)),
            _resident((1, hv)),
            _resident(w_out.shape),
            _resident((1, d)),
            _resident(w_gu.shape),
            _resident(w_down.shape),
        ],
        out_specs=pl.BlockSpec((tm, d), lambda i: (i, 0)),
        out_shape=jax.ShapeDtypeStruct((n, d), F32),
        scratch_shapes=[pltpu.VMEM((tm, d_ff), BF16)],
        compiler_params=_params(("parallel",), 56),
        name="mix_ffn",
    )(x_p, x_s, onorm_p, onorm_s, gate, g_out, w_out, g_ffn, w_gu, w_down)


def _rope_lanes(x, cos, sin_signed, half):
    width = x.shape[-1]
    lane = lax.broadcasted_iota(jnp.int32, x.shape, 1)
    first = (lane % (2 * half)) < half
    rot = jnp.where(first, pltpu.roll(x, width - half, axis=1), pltpu.roll(x, half, axis=1))
    return x * cos + rot * sin_signed


def _unit_rms(x, width):
    ms = jnp.sum(x * x, axis=-1, keepdims=True) * (1.0 / width)
    return x * lax.rsqrt(ms + EPS)


HEAD_PAD = 256
V_PAD = 144


def _mla_proj_kernel(h_ref, gmix_ref, gkv_ref, wdq_ref, gqa_ref, wuq_ref, gqn_ref, gqp_ref,
                     wdkvc_ref, wdkvp_ref, gkva_ref, gkpe_ref, wuk_ref, wuvt_ref, gkn_ref,
                     cos_ref, sin_ref,
                     qcat_ref, kcat_ref, vt_ref, ckv_ref, kpe_ref, *, scale):
    h = h_ref[...]
    r = lax.rsqrt(jnp.mean(h * h, axis=-1, keepdims=True) + EPS)
    xn = ((h * r) * gmix_ref[...]).astype(BF16)
    xk = ((h * r) * gkv_ref[...]).astype(BF16)
    cos = cos_ref[...]
    sin = sin_ref[...]
    half = MLA_ROPE // 2

    ckv = _rms(_dot(xk, wdkvc_ref[...]), gkva_ref[...])
    ckv_ref[...] = ckv
    ap = _dot(xk, wdkvp_ref[...])
    kpe = _rope_lanes(_unit_rms(ap, MLA_ROPE) * gkpe_ref[...], cos, sin, half)
    kpe_ref[...] = kpe[:, :MLA_ROPE]
    kpe_b = kpe.astype(BF16)

    cq = _rms(_dot(xn, wdq_ref[...]), gqa_ref[...]).astype(BF16)
    q = _dot(cq, wuq_ref[...])
    cb = ckv.astype(BF16)
    kn = _dot(cb, wuk_ref[...])
    for hd in range(MLA_HEADS):
        c0 = hd * HEAD_PAD
        qn = _unit_rms(q[:, c0:c0 + MLA_NOPE], MLA_NOPE) * gqn_ref[...]
        qp = _unit_rms(q[:, c0 + MLA_NOPE:c0 + HEAD_PAD], MLA_ROPE) * gqp_ref[...]
        qp = _rope_lanes(qp, cos, sin, half)
        qcat_ref[:, c0:c0 + MLA_NOPE] = (qn * scale).astype(BF16)
        qcat_ref[:, c0 + MLA_NOPE:c0 + HEAD_PAD] = (qp * scale).astype(BF16)
        kh = _unit_rms(kn[:, hd * MLA_NOPE:(hd + 1) * MLA_NOPE], MLA_NOPE) * gkn_ref[...]
        kcat_ref[:, c0:c0 + MLA_NOPE] = kh.astype(BF16)
        kcat_ref[:, c0 + MLA_NOPE:c0 + HEAD_PAD] = kpe_b
    vt = _dot_nt(wuvt_ref[...], cb)
    ones = jnp.ones((V_PAD - MLA_DV, vt.shape[1]), BF16)
    for hd in range(MLA_HEADS):
        vt_ref[hd * V_PAD:hd * V_PAD + MLA_DV, :] = vt[hd * MLA_DV:(hd + 1) * MLA_DV, :].astype(BF16)
        vt_ref[hd * V_PAD + MLA_DV:(hd + 1) * V_PAD, :] = ones


def _mla_proj(h, weights, cos_tab, sin_tab, n_prompt_tiles, tiles_per_seq, scale):
    n, d = h.shape
    tm = TOKEN_TILE
    hc = MLA_HEADS * HEAD_PAD
    hn = MLA_HEADS * V_PAD
    kv_lora = weights[7].shape[1]

    def tab_map(i):
        return (jnp.where(i < n_prompt_tiles, i % tiles_per_seq, tiles_per_seq), 0)

    row = lambda i: (i, 0)
    in_specs = [pl.BlockSpec((tm, d), row)] + [_resident(w.shape) for w in weights]
    in_specs += [pl.BlockSpec((tm, LANES), tab_map), pl.BlockSpec((tm, LANES), tab_map)]
    return pl.pallas_call(
        functools.partial(_mla_proj_kernel, scale=scale),
        grid=(n // tm,),
        in_specs=in_specs,
        out_specs=[
            pl.BlockSpec((tm, hc), row),
            pl.BlockSpec((tm, hc), row),
            pl.BlockSpec((hn, tm), lambda i: (0, i)),
            pl.BlockSpec((tm, kv_lora), row),
            pl.BlockSpec((tm, MLA_ROPE), row),
        ],
        out_shape=[
            jax.ShapeDtypeStruct((n, hc), BF16),
            jax.ShapeDtypeStruct((n, hc), BF16),
            jax.ShapeDtypeStruct((hn, n), BF16),
            jax.ShapeDtypeStruct((n, kv_lora), F32),
            jax.ShapeDtypeStruct((n, MLA_ROPE), F32),
        ],
        compiler_params=_params(("parallel",), 48),
        name="mla_proj",
    )(h, *weights, cos_tab, sin_tab)


def _prompt_attn_kernel(qi_ref, ki_ref, q_ref, k_ref, vt_ref, o_ref, *scratch, tile, qblock):
    m_refs = scratch[:MLA_HEADS]
    acc_refs = scratch[MLA_HEADS:2 * MLA_HEADS]
    s_scr = scratch[2 * MLA_HEADS:2 * MLA_HEADS + 2]
    mx_scr = scratch[2 * MLA_HEADS + 2:]
    qi = qi_ref[pl.program_id(1)]
    ki = ki_ref[pl.program_id(1)]

    @pl.when(ki == 0)
    def _():
        for h in range(MLA_HEADS):
            m_refs[h][...] = jnp.full_like(m_refs[h], -jnp.inf)
            acc_refs[h][...] = jnp.zeros_like(acc_refs[h])

    def key_tile(diagonal):
        if diagonal:
            key = lax.broadcasted_iota(jnp.int32, (tile, tile), 0)
            qry = lax.broadcasted_iota(jnp.int32, (tile, tile), 1)
        def scores(h):
            cols = slice(h * HEAD_PAD, (h + 1) * HEAD_PAD)
            s = _dot_nt(k_ref[:, cols], q_ref[:, cols])
            if diagonal:
                s = jnp.where(key <= qry, s, -jnp.inf)
            s_scr[h % 2][...] = s
            mx_scr[h % 2][...] = jnp.max(s, axis=0, keepdims=True)

        def fold(h):
            m_old = m_refs[h][...]
            m_new = jnp.maximum(m_old, mx_scr[h % 2][...])
            m_refs[h][...] = m_new
            p = jnp.exp2(s_scr[h % 2][...] - m_new).astype(BF16)
            acc_refs[h][...] = jnp.exp2(m_old - m_new) * acc_refs[h][...] + _dot(
                vt_ref[h * V_PAD:(h + 1) * V_PAD, :], p)

        scores(0)
        for h in range(MLA_HEADS):
            if h + 1 < MLA_HEADS:
                scores(h + 1)
            fold(h)

    @pl.when(ki < qi)
    def _():
        key_tile(False)

    @pl.when(ki == qi)
    def _():
        key_tile(True)
        outs = []
        for h in range(MLA_HEADS):
            blk = acc_refs[h][...]
            outs.append(blk[:MLA_DV] / blk[MLA_DV:MLA_DV + 1])
        o_ref[...] = jnp.concatenate(outs, axis=0).T


def _prompt_attn(qcat, kcat, vt, batch, seq):
    t = ATTN_TILE
    per_seq = seq // t
    hc = qcat.shape[1]
    hn = MLA_HEADS * MLA_DV
    hvp = vt.shape[0]
    pairs = [(qi, ki) for qi in range(per_seq) for ki in range(qi + 1)]
    qi_tab = jnp.asarray([p[0] for p in pairs], jnp.int32)
    ki_tab = jnp.asarray([p[1] for p in pairs], jnp.int32)
    qrow = lambda b, s, qt, kt: (b * per_seq + qt[s], 0)
    krow = lambda b, s, qt, kt: (b * per_seq + kt[s], 0)
    vcol = lambda b, s, qt, kt: (0, b * per_seq + kt[s])
    grid_spec = pltpu.PrefetchScalarGridSpec(
        num_scalar_prefetch=2,
        grid=(batch, len(pairs)),
        in_specs=[
            pl.BlockSpec((t, hc), qrow),
            pl.BlockSpec((t, hc), krow),
            pl.BlockSpec((hvp, t), vcol),
        ],
        out_specs=pl.BlockSpec((t, hn), qrow),
        scratch_shapes=([pltpu.VMEM((1, t), F32)] * MLA_HEADS + [pltpu.VMEM((V_PAD, t), F32)] * MLA_HEADS
                        + [pltpu.VMEM((t, t), F32)] * 2 + [pltpu.VMEM((1, t), F32)] * 2),
    )
    return pl.pallas_call(
        functools.partial(_prompt_attn_kernel, tile=t, qblock=ATTN_QBLOCK),
        grid_spec=grid_spec,
        out_shape=jax.ShapeDtypeStruct((batch * seq, hn), F32),
        compiler_params=_params(("parallel", "arbitrary"), 40),
        name="prompt_attn",
    )(qi_tab, ki_tab, qcat, kcat, vt)


def _absorb_q_kernel(qcat_ref, gkn_ref, wuk_ref, qa_ref, qps_ref):
    lora = wuk_ref.shape[0]
    for h in range(MLA_HEADS):
        c0 = h * HEAD_PAD
        qg = (qcat_ref[:, c0:c0 + MLA_NOPE].astype(F32) * gkn_ref[...]).astype(BF16)
        qa_ref[:, h * lora:(h + 1) * lora] = _dot_nt(qg, wuk_ref[:, h * MLA_NOPE:(h + 1) * MLA_NOPE])
        qps_ref[:, h * LANES:(h + 1) * LANES] = qcat_ref[:, c0 + MLA_NOPE:c0 + HEAD_PAD].astype(F32)


def _absorb_q(qcat, gkn, wuk, n_prompt, n_sample):
    hc = qcat.shape[1]
    lora = wuk.shape[0]
    tm = min(256, n_sample)
    base = n_prompt // tm
    return pl.pallas_call(
        _absorb_q_kernel,
        grid=(n_sample // tm,),
        in_specs=[
            pl.BlockSpec((tm, hc), lambda i: (base + i, 0)),
            _resident(gkn.shape),
            _resident(wuk.shape),
        ],
        out_specs=[
            pl.BlockSpec((tm, MLA_HEADS * lora), lambda i: (i, 0)),
            pl.BlockSpec((tm, MLA_HEADS * LANES), lambda i: (i, 0)),
        ],
        out_shape=[
            jax.ShapeDtypeStruct((n_sample, MLA_HEADS * lora), F32),
            jax.ShapeDtypeStruct((n_sample, MLA_HEADS * LANES), F32),
        ],
        compiler_params=_params(("parallel",), 32),
        name="absorb_q",
    )(qcat, gkn, wuk)


def _sample_attn_kernel(pt_ref, qa_ref, qp_ref, cnew_ref, pnew_ref, wukt_ref, wuv_ref, cache_c_ref, cache_pt_ref,
                        o_ref, cbuf, pbuf, sem, m_ref, l_ref, acc_ref, lhs_s, qa_s, qp_s, cnew_s, pnew_s,
                        cbb0, cbb1, s_scr0, s_scr1, *, n_pages, ppc, page, dec_seq, lora, key_block):
    b = pl.program_id(0)
    nb = pl.num_programs(0)
    n_chunks = n_pages // ppc
    hq = MLA_HEADS * dec_seq
    n_proj = MLA_HEADS * MLA_NOPE

    def copies(seq, chunk, slot):
        out = []
        for p in range(ppc):
            pg = pt_ref[seq, chunk * ppc + p]
            out.append(pltpu.make_async_copy(cache_c_ref.at[pg], cbuf.at[slot, pl.ds(p * page, page)], sem.at[0, slot]))
            out.append(pltpu.make_async_copy(cache_pt_ref.at[pg], pbuf.at[slot, :, pl.ds(p * page, page)],
                                             sem.at[1, slot]))
        return out

    @pl.when(b == 0)
    def _():
        for cp in copies(0, 0, 0):
            cp.start()
        lhs_s[0:n_proj, :] = wukt_ref[...]

    for h in range(MLA_HEADS):
        qa_s[h * dec_seq:(h + 1) * dec_seq, :] = qa_ref[:, h * lora:(h + 1) * lora]
        qp_s[h * dec_seq:(h + 1) * dec_seq, :] = qp_ref[:, h * LANES:(h + 1) * LANES]
    lhs_s[n_proj:n_proj + hq, :] = qa_s[...].astype(BF16)
    qp = qp_s[...][:, :MLA_ROPE].astype(BF16)
    m_ref[...] = jnp.full_like(m_ref, -jnp.inf)
    l_ref[...] = jnp.zeros_like(l_ref)
    acc_ref[...] = jnp.zeros_like(acc_ref)

    def latent_scores(cb):
        t = cb.shape[0]
        kb = min(key_block, t)
        blocks = []
        for j in range(t // kb):
            kt = _dot_nt(lhs_s[...], cb[j * kb:(j + 1) * kb, :])
            parts = []
            for h in range(MLA_HEADS):
                kh = kt[h * MLA_NOPE:(h + 1) * MLA_NOPE, :]
                ms = jnp.sum(kh * kh, axis=0, keepdims=True) * (1.0 / MLA_NOPE)
                parts.append(kt[n_proj + h * dec_seq:n_proj + (h + 1) * dec_seq, :] * lax.rsqrt(ms + EPS))
            blocks.append(jnp.concatenate(parts, axis=0))
        return blocks[0] if len(blocks) == 1 else jnp.concatenate(blocks, axis=1)

    cbb = (cbb0, cbb1)
    s_scr = (s_scr0, s_scr1)

    def project(slot, par):
        cb = cbuf[slot].astype(BF16)
        cbb[par][...] = cb
        s_scr[par][...] = latent_scores(cb) + _dot(qp, pbuf[slot].astype(BF16))

    def absorb(*blocks):
        m_old = m_ref[...]
        m_new = m_old
        for s, _ in blocks:
            m_new = jnp.maximum(m_new, jnp.max(s, axis=-1, keepdims=True))
        alpha = jnp.exp2(m_old - m_new)
        l_new = alpha * l_ref[...]
        acc = alpha * acc_ref[...]
        for s, cb in blocks:
            p = jnp.exp2(s - m_new)
            l_new = l_new + jnp.sum(p, axis=-1, keepdims=True)
            acc = acc + _dot(p.astype(BF16), cb)
        m_ref[...] = m_new
        l_ref[...] = l_new
        acc_ref[...] = acc

    def fetch_next(c, slot):
        if c + 1 < n_chunks:
            for cp in copies(b, c + 1, 1 - slot):
                cp.start()
        else:
            @pl.when(b + 1 < nb)
            def _():
                for cp in copies(b + 1, 0, 1 - slot):
                    cp.start()

        for cp in copies(b, c, slot):
            cp.wait()

    for c in range(n_chunks):
        slot = (b * n_chunks + c) % 2 if n_chunks % 2 else c % 2
        fetch_next(c, slot)
        project(slot, c % 2)
        if c:
            absorb((s_scr[(c - 1) % 2][...], cbb[(c - 1) % 2][...]))

    cnew_s[...] = jnp.zeros_like(cnew_s)
    pnew_s[...] = jnp.zeros_like(pnew_s)
    cnew_s[0:dec_seq, :] = cnew_ref[...]
    pnew_s[0:dec_seq, :] = pnew_ref[...]
    cnew = cnew_s[...].astype(BF16)
    row_tok = lax.broadcasted_iota(jnp.int32, (hq, LANES), 0) % dec_seq
    col_tok = lax.broadcasted_iota(jnp.int32, (hq, LANES), 1)
    s_new = latent_scores(cnew) + _dot_nt(qp, pnew_s[...].astype(BF16))
    s_new = jnp.where(col_tok <= row_tok, s_new, -jnp.inf)
    last = (n_chunks - 1) % 2
    absorb((s_scr[last][...], cbb[last][...]), (s_new, cnew))

    lat = (acc_ref[...] / l_ref[...]).astype(BF16)
    full = _dot(lat, wuv_ref[...])
    for h in range(MLA_HEADS):
        cols = slice(h * MLA_DV, (h + 1) * MLA_DV)
        o_ref[:, cols] = full[h * dec_seq:(h + 1) * dec_seq, cols]


def _sample_attn(page_table, qa, qp, ckv, kpe, wukt, wuv, cache_c, cache_pt, n_prompt, dec_batch, dec_seq):
    n_pages = page_table.shape[1]
    page = cache_c.shape[1]
    lora = cache_c.shape[2]
    ppc = min(PAGES_PER_CHUNK, n_pages)
    hn = MLA_HEADS * MLA_DV
    hq = MLA_HEADS * dec_seq
    base = n_prompt // dec_seq
    srow = lambda b, pt: (b, 0)
    nrow = lambda b, pt: (base + b, 0)
    grid_spec = pltpu.PrefetchScalarGridSpec(
        num_scalar_prefetch=1,
        grid=(dec_batch,),
        in_specs=[
            pl.BlockSpec((dec_seq, MLA_HEADS * lora), srow),
            pl.BlockSpec((dec_seq, MLA_HEADS * LANES), srow),
            pl.BlockSpec((dec_seq, lora), nrow),
            pl.BlockSpec((dec_seq, MLA_ROPE), nrow),
            pl.BlockSpec(wukt.shape, lambda b, pt: (0, 0), pipeline_mode=pl.Buffered(1)),
            pl.BlockSpec(wuv.shape, lambda b, pt: (0, 0), pipeline_mode=pl.Buffered(1)),
            pl.BlockSpec(memory_space=pl.ANY),
            pl.BlockSpec(memory_space=pl.ANY),
        ],
        out_specs=pl.BlockSpec((dec_seq, hn), srow),
        scratch_shapes=[
            pltpu.VMEM((2, ppc * page, lora), F32),
            pltpu.VMEM((2, MLA_ROPE, ppc * page), F32),
            pltpu.SemaphoreType.DMA((2, 2)),
            pltpu.VMEM((hq, 1), F32),
            pltpu.VMEM((hq, 1), F32),
            pltpu.VMEM((hq, lora), F32),
            pltpu.VMEM((MLA_HEADS * MLA_NOPE + hq, lora), BF16),
            pltpu.VMEM((hq, lora), F32),
            pltpu.VMEM((hq, LANES), F32),
            pltpu.VMEM((LANES, lora), F32),
            pltpu.VMEM((LANES, MLA_ROPE), F32),
            pltpu.VMEM((ppc * page, lora), BF16),
            pltpu.VMEM((ppc * page, lora), BF16),
            pltpu.VMEM((hq, ppc * page), F32),
            pltpu.VMEM((hq, ppc * page), F32),
        ],
    )
    return pl.pallas_call(
        functools.partial(_sample_attn_kernel, n_pages=n_pages, ppc=ppc, page=page, dec_seq=dec_seq, lora=lora,
                          key_block=SAMPLE_KEY_BLOCK),
        grid_spec=grid_spec,
        out_shape=jax.ShapeDtypeStruct((dec_batch * dec_seq, hn), F32),
        compiler_params=_params(("arbitrary",), 40),
        name="sample_attn",
    )(page_table, qa, qp, ckv, kpe, wukt, wuv, cache_c, cache_pt)


def _route_kernel(h_ref, op_ref, os_ref, wo_ref, gffn_ref, wr_ref, tri_ref, h3_ref, xf_ref, info_ref, infot_ref,
                  cnt_ref, *rest, n_prompt_tiles):
    carry_ref = rest[-1]
    i = pl.program_id(0)

    @pl.when(i == 0)
    def _():
        carry_ref[...] = jnp.zeros_like(carry_ref)

    for zero_ref in rest[:-1]:
        zero_ref[...] = jnp.zeros_like(zero_ref)

    o = jnp.where(i < n_prompt_tiles, op_ref[...], os_ref[...])
    h3 = h_ref[...] + _dot(o.astype(BF16), wo_ref[...])
    h3_ref[...] = h3
    xf = _rms(h3, gffn_ref[...])
    xf_ref[...] = xf
    logits = _dot(xf.astype(BF16), wr_ref[...])
    lane = lax.broadcasted_iota(jnp.int32, logits.shape, 1).astype(F32)
    lg = jnp.where(lane < N_EXPERTS, logits, -jnp.inf)
    v1 = jnp.max(lg, axis=-1, keepdims=True)
    i1 = jnp.min(jnp.where(lg == v1, lane, float(LANES)), axis=-1, keepdims=True)
    lg2 = jnp.where(lane == i1, -jnp.inf, lg)
    v2 = jnp.max(lg2, axis=-1, keepdims=True)
    i2 = jnp.min(jnp.where(lg2 == v2, lane, float(LANES)), axis=-1, keepdims=True)
    e = jnp.exp(v2 - v1)
    g1 = 1.0 / (1.0 + e)
    g2 = e / (1.0 + e)
    oh1 = lane == i1
    oh2 = lane == i2
    chosen = jnp.where(oh1 | oh2, 1.0, 0.0)
    cum = _dot(tri_ref[...], chosen.astype(BF16))
    before = cum - chosen + carry_ref[...]
    pos1 = jnp.sum(jnp.where(oh1, before, 0.0), axis=-1, keepdims=True)
    pos2 = jnp.sum(jnp.where(oh2, before, 0.0), axis=-1, keepdims=True)
    carry_ref[...] = carry_ref[...] + cum[cum.shape[0] - 1:, :]
    cnt_ref[...] = carry_ref[...]
    info = jnp.where(lane == 0, i1, 0.0)
    info = jnp.where(lane == 1, i2, info)
    info = jnp.where(lane == 2, pos1, info)
    info = jnp.where(lane == 3, pos2, info)
    info = jnp.where(lane == 4, g1, info)
    info = jnp.where(lane == 5, g2, info)
    info_ref[...] = info
    infot_ref[...] = info.T[:8, :]


def _route(h2, o_attn_p, o_attn_s, w_o, g_ffn, w_r, tri, sorted_rows):
    n, d = h2.shape
    tm = TOKEN_TILE
    npt = o_attn_p.shape[0] // tm
    hn = o_attn_p.shape[1]
    row = lambda i: (i, 0)
    steps = n // tm
    zc = 0
    for c in range(steps, 0, -1):
        if sorted_rows % c == 0 and (sorted_rows // c) % 8 == 0 and (sorted_rows // c) * d * 4 <= CAST_SLAB_BYTES:
            zc = c
            break
    zero_specs = [pl.BlockSpec((sorted_rows // zc, d), lambda i: (jnp.minimum(i, zc - 1), 0))] if zc else []
    zero_shapes = [jax.ShapeDtypeStruct((sorted_rows, d), F32)] if zc else []
    outs = pl.pallas_call(
        functools.partial(_route_kernel, n_prompt_tiles=npt),
        grid=(steps,),
        in_specs=[
            pl.BlockSpec((tm, d), row),
            pl.BlockSpec((tm, hn), lambda i: (jnp.minimum(i, npt - 1), 0)),
            pl.BlockSpec((tm, hn), lambda i: (jnp.maximum(i - npt, 0), 0)),
            _resident(w_o.shape),
            _resident((1, d)),
            _resident(w_r.shape),
            _resident(tri.shape),
        ],
        out_specs=[
            pl.BlockSpec((tm, d), row),
            pl.BlockSpec((tm, d), row),
            pl.BlockSpec((tm, LANES), row),
            pl.BlockSpec((8, tm), lambda i: (0, i)),
            pl.BlockSpec((1, LANES), lambda i: (0, 0)),
        ] + zero_specs,
        out_shape=[
            jax.ShapeDtypeStruct((n, d), F32),
            jax.ShapeDtypeStruct((n, d), F32),
            jax.ShapeDtypeStruct((n, LANES), F32),
            jax.ShapeDtypeStruct((8, n), F32),
            jax.ShapeDtypeStruct((1, LANES), F32),
        ] + zero_shapes,
        scratch_shapes=[pltpu.VMEM((1, LANES), F32)],
        compiler_params=_params(("arbitrary",), 48),
        name="route",
    )(h2, o_attn_p, o_attn_s, w_o, g_ffn, w_r, tri)
    return tuple(outs[:5]) + ((outs[5] if zc else None),)


def _scatter_kernel(dest_ref, x_ref, xs_in_ref, *rest, tm):
    del xs_in_ref
    n_w = (len(rest) - 2) // 2
    w_in, xs_ref, w_out, sem = rest[:n_w], rest[n_w], rest[n_w + 1:2 * n_w + 1], rest[-1]

    def issue(r, carry):
        for k in range(2):
            d = dest_ref[0, 0, k * tm + r]
            pltpu.make_async_copy(x_ref.at[pl.ds(r, 1)], xs_ref.at[pl.ds(d, 1)], sem).start(priority=k)
        return carry

    lax.fori_loop(0, tm, issue, 0, unroll=8)

    for src, dst in zip(w_in, w_out):
        dst[...] = src[...].astype(dst.dtype)

    def drain(r, carry):
        for k in range(2):
            pltpu.make_async_copy(x_ref.at[pl.ds(r, 1)], xs_ref.at[pl.ds(0, 1)], sem).wait()
        return carry

    lax.fori_loop(0, tm, drain, 0, unroll=8)


CAST_SLAB_BYTES = 8 * MIB


def _cast_steps(steps, mats):
    for c in range(steps, 0, -1):
        if all(m.shape[0] % c == 0 and (m.shape[0] // c) % 16 == 0
               and (m.shape[0] // c) * m.shape[1] * 4 <= CAST_SLAB_BYTES for m in mats):
            return c
    return 0


def _scatter_rows(dest, xf, xs0, mats):
    n, d = xf.shape
    tm = TOKEN_TILE
    steps = n // tm
    c = _cast_steps(steps, mats)
    if c == 0:
        fused = []
    else:
        fused = list(mats)
    slab = lambda i: (jnp.minimum(i, c - 1), 0)
    w_specs = [pl.BlockSpec((m.shape[0] // c, m.shape[1]), slab) for m in fused]
    outs = pl.pallas_call(
        functools.partial(_scatter_kernel, tm=tm),
        grid=(steps,),
        in_specs=[
            pl.BlockSpec((1, 1, 2 * tm), lambda i: (i, 0, 0), memory_space=pltpu.SMEM),
            pl.BlockSpec((tm, d), lambda i: (i, 0)),
            pl.BlockSpec(memory_space=pl.ANY),
        ] + w_specs,
        out_specs=[pl.BlockSpec(memory_space=pl.ANY)] + w_specs,
        out_shape=[jax.ShapeDtypeStruct(xs0.shape, xs0.dtype)]
        + [jax.ShapeDtypeStruct(m.shape, BF16) for m in fused],
        scratch_shapes=[pltpu.SemaphoreType.DMA(())],
        input_output_aliases={2: 0},
        compiler_params=_params(("arbitrary",), 48),
        name="scatter_rows",
    )(dest, xf, xs0, *fused)
    cast = list(outs[1:]) if fused else [m.astype(BF16) for m in mats]
    return outs[0], cast


def _experts_kernel(te_ref, tr_ref, x_ref, wg_ref, wu_ref, wd_ref, o_ref, acc_ref, xb_ref, act0, act1, *, sub):
    t = pl.program_id(0)
    c = pl.program_id(1)
    n_sub = x_ref.shape[0] // sub
    act_s = (act0, act1)

    @pl.when(c == 0)
    def _():
        xb_ref[...] = x_ref[...].astype(BF16)
        acc_ref[...] = jnp.zeros_like(acc_ref)

    rows = tr_ref[t]
    n_part = x_ref.shape[0] // EXPERT_PART
    mostly_full = rows > (n_part - 1) * EXPERT_PART

    @pl.when((rows > 0) & jnp.logical_not(mostly_full))
    def _():
        for sb in range(n_part - 1):
            r = slice(sb * EXPERT_PART, (sb + 1) * EXPERT_PART)

            @pl.when(sb * EXPERT_PART < rows)
            def _():
                xb = xb_ref[r, :]
                act = (_silu(_dot(xb, wg_ref[0])) * _dot(xb, wu_ref[0])).astype(BF16)
                acc_ref[r, :] += _dot(act, wd_ref[0])

    @pl.when(mostly_full)
    def _():
        def up(i):
            xb = xb_ref[i * sub:(i + 1) * sub, :]
            act_s[i % 2][...] = (_silu(_dot(xb, wg_ref[0])) * _dot(xb, wu_ref[0])).astype(BF16)

        def down(i):
            acc_ref[i * sub:(i + 1) * sub, :] += _dot(act_s[i % 2][...], wd_ref[0])

        up(0)
        for i in range(n_sub):
            if i + 1 < n_sub:
                up(i + 1)
            down(i)

    @pl.when(c == pl.num_programs(1) - 1)
    def _():
        o_ref[...] = acc_ref[...]


def _experts(tile_expert, tile_rows, xs, w_gu, w_down):
    rows, d = xs.shape
    tm = EXPERT_TILE
    fc = FF_CHUNK
    d_ff = w_down.shape[1]
    n_c = d_ff // fc
    n_tiles = rows // tm

    def ceff(t, c, tr):
        return jnp.where(tr[t] > 0, c, n_c - 1)

    grid_spec = pltpu.PrefetchScalarGridSpec(
        num_scalar_prefetch=2,
        grid=(n_tiles, n_c),
        in_specs=[
            pl.BlockSpec((tm, d), lambda t, c, te, tv: (t, 0)),
            pl.BlockSpec((1, d, fc), lambda t, c, te, tv: (te[t], 0, ceff(t, c, tv))),
            pl.BlockSpec((1, d, fc), lambda t, c, te, tv: (te[t], 0, n_c + ceff(t, c, tv))),
            pl.BlockSpec((1, fc, d), lambda t, c, te, tv: (te[t], ceff(t, c, tv), 0)),
        ],
        out_specs=pl.BlockSpec((tm, d), lambda t, c, te, tv: (t, 0)),
        scratch_shapes=[pltpu.VMEM((tm, d), F32), pltpu.VMEM((tm, d), BF16),
                        pltpu.VMEM((EXPERT_SUB, fc), BF16), pltpu.VMEM((EXPERT_SUB, fc), BF16)],
    )
    return pl.pallas_call(
        functools.partial(_experts_kernel, sub=EXPERT_SUB),
        grid_spec=grid_spec,
        out_shape=jax.ShapeDtypeStruct((rows, d), F32),
        compiler_params=_params(("arbitrary", "arbitrary"), 52),
        name="experts",
    )(tile_expert, tile_rows, xs, w_gu, w_gu, w_down)


def _combine_kernel(dest_ref, h_ref, info_ref, ys_ref, yp_ref, ysm_ref, buf, sem, *, tm, n_prompt_tiles):
    def issue(r, carry):
        for k in range(2):
            d = dest_ref[0, 0, k * tm + r]
            pltpu.make_async_copy(ys_ref.at[pl.ds(d, 1)], buf.at[k, pl.ds(r, 1)], sem).start(priority=k)
        return carry

    lax.fori_loop(0, tm, issue, 0, unroll=8)

    def drain(r, carry):
        for k in range(2):
            pltpu.make_async_copy(ys_ref.at[pl.ds(0, 1)], buf.at[k, pl.ds(r, 1)], sem).wait()
        return carry

    lax.fori_loop(0, tm, drain, 0, unroll=8)
    info = info_ref[...]
    y = h_ref[...] + info[:, 4:5] * buf[0] + info[:, 5:6] * buf[1]

    @pl.when(pl.program_id(0) < n_prompt_tiles)
    def _():
        yp_ref[...] = y

    @pl.when(pl.program_id(0) >= n_prompt_tiles)
    def _():
        ysm_ref[...] = y


def _combine(dest, h3, info, ys, n_prompt):
    n, d = h3.shape
    tm = TOKEN_TILE
    npt = n_prompt // tm
    return pl.pallas_call(
        functools.partial(_combine_kernel, tm=tm, n_prompt_tiles=npt),
        grid=(n // tm,),
        in_specs=[
            pl.BlockSpec((1, 1, 2 * tm), lambda i: (i, 0, 0), memory_space=pltpu.SMEM),
            pl.BlockSpec((tm, d), lambda i: (i, 0)),
            pl.BlockSpec((tm, LANES), lambda i: (i, 0)),
            pl.BlockSpec(memory_space=pl.ANY),
        ],
        out_specs=[
            pl.BlockSpec((tm, d), lambda i: (jnp.minimum(i, npt - 1), 0)),
            pl.BlockSpec((tm, d), lambda i: (jnp.maximum(i - npt, 0), 0)),
        ],
        out_shape=[
            jax.ShapeDtypeStruct((n_prompt, d), F32),
            jax.ShapeDtypeStruct((n - n_prompt, d), F32),
        ],
        scratch_shapes=[pltpu.VMEM((2, tm, d), F32), pltpu.SemaphoreType.DMA(())],
        compiler_params=_params(("arbitrary",), 32),
        name="combine",
    )(dest, h3, info, ys)


def _rope_tables(pos, dim):
    inv = ROPE_BASE ** (-jnp.arange(0, dim, 2, dtype=F32) / dim)
    ang = pos.astype(F32)[:, None] * inv[None, :]
    return jnp.cos(ang), jnp.sin(ang)


def kernel(x_prompt, x_sample, state_ret, cache_ckv, cache_kpe, page_table, g_pre_mix, g_pre_ffn, ret_w_in, ret_g_out, ret_w_out, mla_g_kv_in, mla_w_dkv, mla_g_kv_a, mla_g_kpe, mla_w_uk, mla_w_uv, mla_g_kn, mla_w_dq, mla_g_qa, mla_w_uq, mla_g_qn, mla_g_qp, mla_w_o, ffn_w_gu, ffn_w_down, moe_w_router, moe_w_gu, moe_w_down):
    batch, seq, d = x_prompt.shape
    dec_batch, dec_seq, _ = x_sample.shape
    n_prompt = batch * seq
    n_sample = dec_batch * dec_seq
    n = n_prompt + n_sample
    tm = TOKEN_TILE
    past = page_table.shape[1] * cache_ckv.shape[1]
    hv = ret_g_out.shape[1]
    hk = (ret_w_in.shape[2] - 2 * hv) // 2
    kv_lora = mla_g_kv_a.shape[0]
    assert seq % tm == 0 and n_sample % tm == 0 and seq % ATTN_TILE == 0 and seq % RET_ROWS == 0
    n_prompt_tiles = n_prompt // tm
    tiles_per_seq = seq // tm

    x_p = x_prompt.reshape(n_prompt, d)
    x_s = x_sample.reshape(n_sample, d)

    pos = jnp.concatenate([jnp.arange(seq), past + (jnp.arange(tm) % dec_seq)])
    cos_r, sin_r = _rope_tables(pos, hk // RET_HEADS)
    cos_m, sin_m = _rope_tables(pos, MLA_ROPE)
    cos_m = jnp.tile(cos_m, (1, 2 * LANES // MLA_ROPE))
    sin_m = jnp.tile(jnp.concatenate([-sin_m, sin_m], axis=-1), (1, LANES // MLA_ROPE))

    q, k, v, gate = _ret_in(x_p, x_s, g_pre_mix[0][None], ret_w_in[0].astype(BF16), cos_r, sin_r,
                            n_prompt_tiles, tiles_per_seq, hk, hv)
    onorm_p, s_prompt = _ret_prompt(q, k, v, _decay_consts(RET_CHUNK, RET_CHUNK), batch, seq)
    seqs = 16 // dec_seq if dec_seq < 16 else 1
    onorm_s, s_sample = _ret_sample(q, k, v, state_ret[0], _decay_consts(dec_seq, seqs * dec_seq),
                                    n_prompt, dec_batch, dec_seq)
    h2 = _mix_ffn(x_p, x_s, onorm_p, onorm_s, gate, ret_g_out[0][None], ret_w_out[0].astype(BF16),
                  g_pre_ffn[0][None], ffn_w_gu[0].astype(BF16), ffn_w_down[0].astype(BF16))

    q_lora = mla_w_dq.shape[2]
    head_dim = MLA_NOPE + MLA_ROPE
    w_uq = jnp.pad(mla_w_uq[0].reshape(q_lora, MLA_HEADS, head_dim), ((0, 0), (0, 0), (0, HEAD_PAD - head_dim)))
    w_uq = w_uq.reshape(q_lora, MLA_HEADS * HEAD_PAD).astype(BF16)
    w_dkv_c = mla_w_dkv[:, :kv_lora].astype(BF16)
    w_dkv_p = jnp.pad(mla_w_dkv[:, kv_lora:], ((0, 0), (0, LANES - MLA_ROPE))).astype(BF16)
    pad_rope = lambda g: jnp.pad(g, (0, LANES - MLA_ROPE))[None]
    weights = [
        g_pre_mix[1][None], mla_g_kv_in[None], mla_w_dq[0].astype(BF16), mla_g_qa[0][None], w_uq,
        mla_g_qn[0][None], pad_rope(mla_g_qp[0]),
        w_dkv_c, w_dkv_p, mla_g_kv_a[None], pad_rope(mla_g_kpe),
        mla_w_uk.astype(BF16), mla_w_uv.T.astype(BF16), mla_g_kn[None],
    ]
    scale = float(head_dim ** -0.5 * 1.4426950408889634)
    qcat, kcat, vt, ckv, kpe = _mla_proj(h2, weights, cos_m, sin_m, n_prompt_tiles, tiles_per_seq, scale)
    o_attn_p = _prompt_attn(qcat, kcat, vt, batch, seq)
    qa, qps = _absorb_q(qcat, mla_g_kn[None], mla_w_uk.astype(BF16), n_prompt, n_sample)
    o_attn_s = _sample_attn(page_table, qa, qps, ckv, kpe, mla_w_uk.T.astype(BF16), mla_w_uv.astype(BF16),
                            cache_ckv, cache_kpe.transpose(0, 2, 1), n_prompt, dec_batch, dec_seq)

    w_r = jnp.pad(moe_w_router[0], ((0, 0), (0, LANES - N_EXPERTS))).astype(BF16)
    tri = (jnp.arange(tm)[:, None] >= jnp.arange(tm)[None, :]).astype(BF16)
    te = EXPERT_TILE
    n_tiles = (2 * n) // te + N_EXPERTS
    h3, xf, info, info_t, counts, xs0 = _route(h2, o_attn_p, o_attn_s, mla_w_o[0].astype(BF16),
                                               g_pre_ffn[1][None], w_r, tri, n_tiles * te)
    if xs0 is None:
        xs0 = jnp.zeros((n_tiles * te, d), F32)

    counts = counts[0, :N_EXPERTS].astype(jnp.int32)
    tiles_e = (counts + te - 1) // te
    tile_end = jnp.cumsum(tiles_e)
    offsets = (tile_end - tiles_e) * te
    tile_ids = jnp.arange(n_tiles, dtype=jnp.int32)
    tile_valid = tile_ids < tile_end[-1]
    tile_expert = jnp.minimum(jnp.sum(tile_ids[:, None] >= tile_end[None, :], axis=1), N_EXPERTS - 1)
    first_tile = jnp.zeros_like(tile_ids)
    count_of = jnp.zeros_like(tile_ids)
    for e in range(N_EXPERTS):
        first_tile = jnp.where(tile_expert == e, tile_end[e] - tiles_e[e], first_tile)
        count_of = jnp.where(tile_expert == e, counts[e], count_of)
    tile_rows = jnp.where(tile_valid, jnp.clip(count_of - (tile_ids - first_tile) * te, 0, te), 0).astype(jnp.int32)
    last_expert = jnp.sum((tile_end[-1] - 1) >= tile_end).astype(jnp.int32)
    tile_expert = jnp.where(tile_valid, tile_expert, last_expert).astype(jnp.int32)
    idx = info_t[0:2].astype(jnp.int32)
    base = jnp.zeros_like(idx)
    for e in range(N_EXPERTS):
        base = jnp.where(idx == e, offsets[e], base)
    dest = base + info_t[2:4].astype(jnp.int32)
    dest = dest.reshape(2, n // tm, tm).transpose(1, 0, 2).reshape(n // tm, 1, 2 * tm)

    w_gu, w_down = moe_w_gu[0], moe_w_down[0]
    xs, (w_gu_b, w_down_b) = _scatter_rows(dest, xf, xs0,
                                           [w_gu.reshape(-1, w_gu.shape[-1]), w_down.reshape(-1, w_down.shape[-1])])
    ys = _experts(tile_expert, tile_rows, xs, w_gu_b.reshape(w_gu.shape), w_down_b.reshape(w_down.shape))
    y_p, y_s = _combine(dest, h3, info, ys, n_prompt)

    return (
        y_p.reshape(batch, seq, d),
        y_s.reshape(dec_batch, dec_seq, d),
        s_prompt[None],
        s_sample[None],
        ckv[:n_prompt].reshape(batch, seq, kv_lora),
        kpe[:n_prompt].reshape(batch, seq, MLA_ROPE),
        ckv[n_prompt:].reshape(dec_batch, dec_seq, kv_lora),
        kpe[n_prompt:].reshape(dec_batch, dec_seq, MLA_ROPE),
    )
```
